```python
import jax, jax.numpy as jnp
from jax import lax
import numpy as np

D_MODEL = 1024
BATCH = 8
SEQ = 2048
DEPTH = 4

N_BRANCH = 4
MIX_WIDTH = D_MODEL // 2
CONF_KERNEL = 31
POOL_WINDOWS = (2, 4, 8, 16)
POOL_GROUPS = len(POOL_WINDOWS)
POOL_GROUP_WIDTH = MIX_WIDTH // POOL_GROUPS
SC_KERNEL = 3
GMLP_CHUNK = 128
GMLP_GROUPS = 4
GMLP_GROUP_WIDTH = MIX_WIDTH // GMLP_GROUPS
D_FF = 4 * D_MODEL
PLE_DIM = 256
EPS = 1e-6

COLS_A = 2 * MIX_WIDTH
COLS_B = MIX_WIDTH
COLS_C = 3 * MIX_WIDTH
COLS_D = 2 * MIX_WIDTH
COLS_G = N_BRANCH * D_MODEL
COLS_IN = COLS_A + COLS_B + COLS_C + COLS_D + COLS_G
SPLITS = (COLS_A, COLS_A + COLS_B, COLS_A + COLS_B + COLS_C, COLS_A + COLS_B + COLS_C + COLS_D)

kernel_name = "hybrid_conv_pool_shortconv_gmlp_block"


def rms_norm(x, g):
    xf = x.astype(jnp.float32)
    y = xf * lax.rsqrt(jnp.mean(xf * xf, axis=-1, keepdims=True) + EPS)
    return (y * g.astype(jnp.float32)).astype(x.dtype)


def layer_norm(x, g, b):
    xf = x.astype(jnp.float32)
    mu = jnp.mean(xf, axis=-1, keepdims=True)
    var = jnp.mean(jnp.square(xf - mu), axis=-1, keepdims=True)
    y = (xf - mu) * lax.rsqrt(var + EPS)
    return (y * g.astype(jnp.float32) + b.astype(jnp.float32)).astype(x.dtype)


def causal_depthwise_conv(x, w):
    K, C = w.shape
    return lax.conv_general_dilated(
        x, w[:, None, :].astype(x.dtype), window_strides=(1,), padding=[(K - 1, 0)],
        dimension_numbers=('NWC', 'WIO', 'NWC'), feature_group_count=C)


def multiscale_pool(u, pool_w, pool_scale):
    S = u.shape[1]
    c = jnp.cumsum(u.astype(jnp.float32), axis=1)
    pos = jnp.arange(S, dtype=jnp.float32)[:, None] + 1.0
    outs = []
    for gi, w in enumerate(POOL_WINDOWS):
        sl = slice(gi * POOL_GROUP_WIDTH, (gi + 1) * POOL_GROUP_WIDTH)
        cg = c[:, :, sl]
        c_shift = jnp.pad(cg, ((0, 0), (w, 0), (0, 0)))[:, :S]
        mean = (cg - c_shift) / jnp.minimum(pos, float(w))
        outs.append(mean.astype(u.dtype) - u[:, :, sl])
    pooled = jnp.stack(outs, axis=2)
    mixed = jnp.einsum('bsgc,gcd->bsgd', pooled, pool_w)
    return mixed.reshape(u.shape) * pool_scale


def spatial_gating(v, ws, bs):
    B, S, _ = v.shape
    n = S // GMLP_CHUNK
    mask = jnp.tril(jnp.ones((GMLP_CHUNK, GMLP_CHUNK), dtype=bool))
    ws_m = jnp.where(mask[None], ws, jnp.zeros_like(ws))
    vc = v.reshape(B, n, GMLP_CHUNK, GMLP_GROUPS, GMLP_GROUP_WIDTH)
    out = jnp.einsum('gts,bnsgc->bntgc', ws_m, vc) + bs.T[:, :, None]
    return out.reshape(B, S, MIX_WIDTH)


def mixer_block(h, w_in, conf_dw, conf_dw_b, conf_ln_g, conf_ln_b, pool_w, pool_scale,
                sc_conv, gmlp_ln_g, gmlp_ln_b, gmlp_ws, gmlp_bs, w_branch, w_out):
    B, S, _ = h.shape
    proj = h @ w_in
    a_in, pool_in, sc_in, g_in, gate_in = jnp.split(proj, SPLITS, axis=-1)
    a, a_gate = jnp.split(a_in, 2, axis=-1)
    ya = causal_depthwise_conv(a * jax.nn.sigmoid(a_gate), conf_dw) + conf_dw_b
    ya = jax.nn.silu(layer_norm(ya, conf_ln_g, conf_ln_b))
    yb = multiscale_pool(pool_in, pool_w, pool_scale)
    bg, cg, hx = jnp.split(sc_in, 3, axis=-1)
    yc = bg * causal_depthwise_conv(cg * hx, sc_conv)
    u, v = jnp.split(g_in, 2, axis=-1)
    yd = u * spatial_gating(layer_norm(v, gmlp_ln_g, gmlp_ln_b), gmlp_ws, gmlp_bs)
    branches = jnp.stack([ya, yb, yc, yd], axis=2)
    z = jnp.einsum('bskw,kwd->bskd', branches, w_branch)
    gates = jax.nn.sigmoid(gate_in.reshape(B, S, N_BRANCH, D_MODEL))
    merged = jnp.sum(gates * z, axis=2)
    return merged @ w_out


def setup_inputs(seed: int = 0) -> dict:
    key = jax.random.key(seed)
    ks = jax.random.split(key, 26)
    f = jnp.float32
    nrm = lambda k, shape, fan: jax.random.normal(k, shape, f) * (fan ** -0.5)
    gain = lambda k, shape: 1.0 + 0.02 * jax.random.normal(k, shape, f)
    small = lambda k, shape: 0.02 * jax.random.normal(k, shape, f)
    return {
        "x": jax.random.normal(ks[0], (BATCH, SEQ, D_MODEL), f),
        "p": jax.random.normal(ks[1], (DEPTH, BATCH, SEQ, PLE_DIM), f),
        "norm_mix": gain(ks[2], (DEPTH, D_MODEL)),
        "w_in": nrm(ks[3], (DEPTH, D_MODEL, COLS_IN), D_MODEL),
        "conf_dw": nrm(ks[4], (DEPTH, CONF_KERNEL, MIX_WIDTH), CONF_KERNEL),
        "conf_dw_b": small(ks[5], (DEPTH, MIX_WIDTH)),
        "conf_ln_g": gain(ks[6], (DEPTH, MIX_WIDTH)),
        "conf_ln_b": small(ks[7], (DEPTH, MIX_WIDTH)),
        "pool_w": nrm(ks[8], (DEPTH, POOL_GROUPS, POOL_GROUP_WIDTH, POOL_GROUP_WIDTH), POOL_GROUP_WIDTH),
        "pool_scale": gain(ks[9], (DEPTH, MIX_WIDTH)),
        "sc_conv": nrm(ks[10], (DEPTH, SC_KERNEL, MIX_WIDTH), SC_KERNEL),
        "gmlp_ln_g": gain(ks[11], (DEPTH, MIX_WIDTH)),
        "gmlp_ln_b": small(ks[12], (DEPTH, MIX_WIDTH)),
        "gmlp_ws": nrm(ks[13], (DEPTH, GMLP_GROUPS, GMLP_CHUNK, GMLP_CHUNK), GMLP_CHUNK),
        "gmlp_bs": gain(ks[14], (DEPTH, GMLP_GROUPS, GMLP_CHUNK)),
        "w_branch": nrm(ks[15], (DEPTH, N_BRANCH, MIX_WIDTH, D_MODEL), MIX_WIDTH),
        "w_out": nrm(ks[16], (DEPTH, D_MODEL, D_MODEL), D_MODEL),
        "norm_mlp": gain(ks[17], (DEPTH, D_MODEL)),
        "w_up": nrm(ks[18], (DEPTH, D_MODEL, D_FF), D_MODEL),
        "w_down": nrm(ks[19], (DEPTH, D_FF, D_MODEL), D_FF),
        "norm_ple": gain(ks[20], (DEPTH, D_MODEL)),
        "w_ple": nrm(ks[21], (DEPTH, PLE_DIM, D_MODEL), PLE_DIM),
        "w_ple_gate": nrm(ks[22], (DEPTH, D_MODEL, D_MODEL), D_MODEL),
        "norm_final": gain(ks[23], (D_MODEL,)),
    }


def reference(x, p, norm_mix, w_in, conf_dw, conf_dw_b, conf_ln_g, conf_ln_b, pool_w, pool_scale,
              sc_conv, gmlp_ln_g, gmlp_ln_b, gmlp_ws, gmlp_bs, w_branch, w_out,
              norm_mlp, w_up, w_down, norm_ple, w_ple, w_ple_gate, norm_final):
    for i in range(DEPTH):
        h = rms_norm(x, norm_mix[i])
        x = x + mixer_block(h, w_in[i], conf_dw[i], conf_dw_b[i], conf_ln_g[i], conf_ln_b[i],
                            pool_w[i], pool_scale[i], sc_conv[i], gmlp_ln_g[i], gmlp_ln_b[i],
                            gmlp_ws[i], gmlp_bs[i], w_branch[i], w_out[i])
        h = rms_norm(x, norm_mlp[i])
        x = x + jnp.square(jax.nn.relu(h @ w_up[i])) @ w_down[i]
        h = rms_norm(x, norm_ple[i])
        x = x + (p[i] @ w_ple[i]) * jax.nn.sigmoid(h @ w_ple_gate[i])
    return rms_norm(x, norm_final)
```

```python
import functools

import jax
import jax.numpy as jnp
from jax import lax
from jax.experimental import pallas as pl
from jax.experimental.pallas import tpu as pltpu

EPS = 1e-6
POOL_WINDOWS = (2, 4, 8, 16)
GMLP_CHUNK = 128
LANES = 128

VMEM_LIMIT_BYTES = 60 * 1024 * 1024

TIME_TILE = 512
TOKEN_TILE = 512
CONV_ROWS = 32
MERGE_COLS = 256
FF_COLS = 512
HALO_A = 32
HALO_P = 16
HALO_C = 8

BF16 = jnp.bfloat16
F32 = jnp.float32


def _dot(a, b):
    return jnp.dot(a, b, preferred_element_type=F32)


def _sigmoid(x):
    return 0.5 * jnp.tanh(0.5 * x) + 0.5


def _rms(x, g):
    ms = jnp.mean(x * x, axis=-1, keepdims=True)
    return x * lax.rsqrt(ms + EPS) * g


def _layer_norm(x, g, b):
    mu = jnp.mean(x, axis=-1, keepdims=True)
    xc = x - mu
    var = jnp.mean(xc * xc, axis=-1, keepdims=True)
    return xc * lax.rsqrt(var + EPS) * g + b


def _carry_halo(t, buf, halo, rows):
    @pl.when(t == 0)
    def _():
        buf[:, 0:halo, :] = jnp.zeros((buf.shape[0], halo, buf.shape[2]), buf.dtype)

    @pl.when(t > 0)
    def _():
        buf[:, 0:halo, :] = buf[:, rows:rows + halo, :]


def _store_lane_groups(buf, halo, val):
    rows = val.shape[0]
    for j in range(buf.shape[0]):
        buf[j, halo:halo + rows, :] = val[:, j * LANES:(j + 1) * LANES]


def _mixer_kernel(x_ref, nrm_ref, win_ref, dw_ref, dwb_ref, lag_ref, lab_ref, pw_ref, ps_ref,
                  sc_ref, ldg_ref, ldb_ref, ws_ref, bst_ref, wbr_ref, wout_ref, o_ref,
                  h_scr, abuf, pbuf, cbuf, y_scr, m_scr):
    t = pl.program_id(1)
    ts = x_ref.shape[1]
    d = x_ref.shape[2]
    n_lane_groups = abuf.shape[0]
    mix = n_lane_groups * LANES
    conv_k = dw_ref.shape[0]
    sc_k = sc_ref.shape[0]
    n_branch = wbr_ref.shape[0]
    col_a, col_b, col_c, col_d = 0, 2 * mix, 3 * mix, 6 * mix
    col_g = 8 * mix

    h_scr[...] = _rms(x_ref[0], nrm_ref[...]).astype(BF16)

    _carry_halo(t, abuf, HALO_A, ts)
    _carry_halo(t, pbuf, HALO_P, ts)
    _carry_halo(t, cbuf, HALO_C, ts)

    pa = _dot(h_scr[...], win_ref[:, col_a:col_a + 2 * mix])
    _store_lane_groups(abuf, HALO_A, pa[:, :mix] * _sigmoid(pa[:, mix:]))

    def conv_chunk(c, carry):
        base = pl.multiple_of(c * CONV_ROWS, CONV_ROWS)
        accs = []
        for j in range(n_lane_groups):
            ls = slice(j * LANES, (j + 1) * LANES)
            acc = jnp.broadcast_to(dwb_ref[:, ls], (CONV_ROWS, LANES))
            for k in range(conv_k):
                off = HALO_A - (conv_k - 1) + k
                acc = acc + dw_ref[k:k + 1, ls] * abuf[j, pl.ds(base + off, CONV_ROWS), :]
            accs.append(acc)
        y = _layer_norm(jnp.concatenate(accs, axis=-1), lag_ref[...], lab_ref[...])
        y_scr[0, pl.ds(base, CONV_ROWS), :] = (y * _sigmoid(y)).astype(BF16)
        return carry

    lax.fori_loop(0, ts // CONV_ROWS, conv_chunk, 0)

    _store_lane_groups(pbuf, HALO_P, _dot(h_scr[...], win_ref[:, col_b:col_b + mix]))
    pos = (t * ts + 1 + lax.broadcasted_iota(jnp.int32, (ts, 1), 0)).astype(F32)
    for gi, w in enumerate(POOL_WINDOWS):
        sl = slice(gi * LANES, (gi + 1) * LANES)
        u = pbuf[gi, HALO_P:HALO_P + ts, :]
        s = u
        for j in range(1, w):
            s = s + pbuf[gi, HALO_P - j:HALO_P - j + ts, :]
        pooled = s / jnp.minimum(pos, float(w)) - u
        mixed = _dot(pooled.astype(BF16), pw_ref[gi]) * ps_ref[:, sl]
        y_scr[1, :, sl] = mixed.astype(BF16)

    pc = _dot(h_scr[...], win_ref[:, col_c:col_c + 3 * mix])
    _store_lane_groups(cbuf, HALO_C, pc[:, mix:2 * mix] * pc[:, 2 * mix:])
    for j in range(n_lane_groups):
        ls = slice(j * LANES, (j + 1) * LANES)
        conv = None
        for k in range(sc_k):
            off = HALO_C - (sc_k - 1) + k
            term = sc_ref[k:k + 1, ls] * cbuf[j, off:off + ts, :]
            conv = term if conv is None else conv + term
        y_scr[2, :, ls] = (pc[:, ls] * conv).astype(BF16)

    pd = _dot(h_scr[...], win_ref[:, col_d:col_d + 2 * mix])
    vn = _layer_norm(pd[:, mix:], ldg_ref[...], ldb_ref[...]).astype(BF16)
    u = pd[:, :mix]
    tri = (lax.broadcasted_iota(jnp.int32, (GMLP_CHUNK, GMLP_CHUNK), 0)
           >= lax.broadcasted_iota(jnp.int32, (GMLP_CHUNK, GMLP_CHUNK), 1))
    n_groups = ws_ref.shape[0]
    gg = mix // n_groups
    for g in range(n_groups):
        wsm = jnp.where(tri, ws_ref[g], 0.0).astype(BF16)
        bcol = bst_ref[:, g:g + 1]
        cs = slice(g * gg, (g + 1) * gg)
        for c in range(ts // GMLP_CHUNK):
            rs = slice(c * GMLP_CHUNK, (c + 1) * GMLP_CHUNK)
            sg = _dot(wsm, vn[rs, cs]) + bcol
            y_scr[3, rs, cs] = (u[rs, cs] * sg).astype(BF16)

    for n in range(d // MERGE_COLS):
        ns = slice(n * MERGE_COLS, (n + 1) * MERGE_COLS)
        merged = None
        for k in range(n_branch):
            gc = col_g + k * d + n * MERGE_COLS
            gate = _sigmoid(_dot(h_scr[...], win_ref[:, gc:gc + MERGE_COLS]))
            term = gate * _dot(y_scr[k], wbr_ref[k, :, ns])
            merged = term if merged is None else merged + term
        m_scr[:, ns] = merged.astype(BF16)
    o_ref[0] = x_ref[0] + _dot(m_scr[...], wout_ref[...])


def _ffn_kernel(x_ref, p_ref, nm_ref, wup_ref, wdn_ref, npl_ref, wple_ref, wpg_ref, nf_ref, o_ref,
                h_scr, u_scr, *, final_norm):
    d_ff = wup_ref.shape[1]
    x = x_ref[...]
    h_scr[...] = _rms(x, nm_ref[...]).astype(BF16)
    for f in range(d_ff // FF_COLS):
        fs = slice(f * FF_COLS, (f + 1) * FF_COLS)
        up = jnp.maximum(_dot(h_scr[...], wup_ref[:, fs]), 0.0)
        u_scr[:, fs] = (up * up).astype(BF16)
    x = x + _dot(u_scr[...], wdn_ref[...])
    h2 = _rms(x, npl_ref[...]).astype(BF16)
    gate = _sigmoid(_dot(h2, wpg_ref[...]))
    x = x + _dot(p_ref[...].astype(BF16), wple_ref[...]) * gate
    if final_norm:
        x = _rms(x, nf_ref[...])
    o_ref[...] = x


def _resident(shape, layer):
    nd = len(shape)
    return pl.BlockSpec((None,) + tuple(shape[1:]), lambda *_: (layer,) + (0,) * (nd - 1),
                        pipeline_mode=pl.Buffered(1))


def _mixer_call(layer, x, norm_mix, w_in, conf_dw, conf_dw_b, conf_ln_g, conf_ln_b, pool_w,
                pool_scale, sc_conv, gmlp_ln_g, gmlp_ln_b, gmlp_ws, gmlp_bs_t, w_branch, w_out):
    b, s, d = x.shape
    mix = conf_dw.shape[-1]
    ts = TIME_TILE
    assert s % ts == 0 and ts % GMLP_CHUNK == 0 and ts % CONV_ROWS == 0
    assert conf_dw.shape[1] - 1 <= HALO_A and max(POOL_WINDOWS) - 1 <= HALO_P
    assert sc_conv.shape[1] - 1 <= HALO_C and d % MERGE_COLS == 0
    assert mix == len(POOL_WINDOWS) * LANES and pool_w.shape[-1] == LANES
    params = (norm_mix, w_in, conf_dw, conf_dw_b, conf_ln_g, conf_ln_b, pool_w, pool_scale,
              sc_conv, gmlp_ln_g, gmlp_ln_b, gmlp_ws, gmlp_bs_t, w_branch, w_out)
    x_spec = pl.BlockSpec((1, ts, d), lambda bi, ti: (bi, ti, 0))
    return pl.pallas_call(
        _mixer_kernel,
        grid=(b, s // ts),
        in_specs=[x_spec] + [_resident(a.shape, layer) for a in params],
        out_specs=x_spec,
        out_shape=jax.ShapeDtypeStruct(x.shape, x.dtype),
        scratch_shapes=[
            pltpu.VMEM((ts, d), BF16),
            pltpu.VMEM((mix // LANES, HALO_A + ts, LANES), F32),
            pltpu.VMEM((mix // LANES, HALO_P + ts, LANES), F32),
            pltpu.VMEM((mix // LANES, HALO_C + ts, LANES), F32),
            pltpu.VMEM((w_branch.shape[1], ts, mix), BF16),
            pltpu.VMEM((ts, d), BF16),
        ],
        compiler_params=pltpu.CompilerParams(
            dimension_semantics=("arbitrary", "arbitrary"),
            vmem_limit_bytes=VMEM_LIMIT_BYTES),
    )(x, *params)


def _ffn_call(layer, final_norm, x2, p, norm_mlp, w_up, w_down, norm_ple, w_ple, w_ple_gate,
              norm_final):
    m, d = x2.shape
    tm = TOKEN_TILE
    ple = p.shape[-1]
    assert m % tm == 0 and w_up.shape[-1] % FF_COLS == 0
    params = (norm_mlp, w_up, w_down, norm_ple, w_ple, w_ple_gate)
    x_spec = pl.BlockSpec((tm, d), lambda i: (i, 0))
    return pl.pallas_call(
        functools.partial(_ffn_kernel, final_norm=final_norm),
        grid=(m // tm,),
        in_specs=[x_spec, pl.BlockSpec((None, tm, ple), lambda i: (layer, i, 0))]
        + [_resident(a.shape, layer) for a in params]
        + [pl.BlockSpec(norm_final.shape, lambda i: (0, 0), pipeline_mode=pl.Buffered(1))],
        out_specs=x_spec,
        out_shape=jax.ShapeDtypeStruct(x2.shape, x2.dtype),
        scratch_shapes=[
            pltpu.VMEM((tm, d), BF16),
            pltpu.VMEM((tm, w_up.shape[-1]), BF16),
        ],
        compiler_params=pltpu.CompilerParams(
            dimension_semantics=("arbitrary",),
            vmem_limit_bytes=VMEM_LIMIT_BYTES),
    )(x2, p, *params, norm_final)


def kernel(x, p, norm_mix, w_in, conf_dw, conf_dw_b, conf_ln_g, conf_ln_b, pool_w, pool_scale, sc_conv, gmlp_ln_g, gmlp_ln_b, gmlp_ws, gmlp_bs, w_branch, w_out, norm_mlp, w_up, w_down, norm_ple, w_ple, w_ple_gate, norm_final):
    depth = w_in.shape[0]
    b, s, d = x.shape
    row = lambda a: a[:, None, :]
    mixer_params = (
        row(norm_mix), w_in.astype(BF16), conf_dw, row(conf_dw_b), row(conf_ln_g), row(conf_ln_b),
        pool_w.astype(BF16), row(pool_scale), sc_conv, row(gmlp_ln_g), row(gmlp_ln_b), gmlp_ws,
        jnp.swapaxes(gmlp_bs, 1, 2), w_branch.astype(BF16), w_out.astype(BF16))
    ffn_params = (row(norm_mlp), w_up.astype(BF16), w_down.astype(BF16), row(norm_ple),
                  w_ple.astype(BF16), w_ple_gate.astype(BF16))
    p2 = p.reshape(depth, b * s, p.shape[-1])
    nf = norm_final[None, :]
    for i in range(depth):
        x = _mixer_call(i, x, *mixer_params)
        x = _ffn_call(i, i == depth - 1, x.reshape(b * s, d), p2, *ffn_params, nf).reshape(b, s, d)
    return x
```

```python
import functools

import jax
import jax.numpy as jnp
from jax import lax
from jax.experimental import pallas as pl
from jax.experimental.pallas import tpu as pltpu

EPS = 1e-6
POOL_WINDOWS = (2, 4, 8, 16)
GMLP_CHUNK = 128
LANES = 128

VMEM_LIMIT_BYTES = 60 * 1024 * 1024

TIME_TILE = 512
TOKEN_TILE = 512
CONV_ROWS = 32
MERGE_COLS = 256
FF_COLS = 512
HALO_A = 32
HALO_P = 16
HALO_C = 8

BF16 = jnp.bfloat16
F32 = jnp.float32


def _dot(a, b):
    return jnp.dot(a, b, preferred_element_type=F32)


def _sigmoid(x):
    return 0.5 * jnp.tanh(0.5 * x) + 0.5


def _rms(x, g):
    ms = jnp.mean(x * x, axis=-1, keepdims=True)
    return x * lax.rsqrt(ms + EPS) * g


def _layer_norm(x, g, b):
    mu = jnp.mean(x, axis=-1, keepdims=True)
    xc = x - mu
    var = jnp.mean(xc * xc, axis=-1, keepdims=True)
    return xc * lax.rsqrt(var + EPS) * g + b


def _carry_halo(t, buf, halo, rows):
    @pl.when(t == 0)
    def _():
        buf[:, 0:halo, :] = jnp.zeros((buf.shape[0], halo, buf.shape[2]), buf.dtype)

    @pl.when(t > 0)
    def _():
        buf[:, 0:halo, :] = buf[:, rows:rows + halo, :]


def _store_lane_groups(buf, halo, val):
    rows = val.shape[0]
    for j in range(buf.shape[0]):
        buf[j, halo:halo + rows, :] = val[:, j * LANES:(j + 1) * LANES]


def _mixer_kernel(x_ref, nrm_ref, win_ref, dw_ref, dwb_ref, lag_ref, lab_ref, pw_ref, ps_ref,
                  sc_ref, ldg_ref, ldb_ref, ws_ref, bst_ref, wbr_ref, wout_ref, o_ref,
                  h_scr, abuf, pbuf, cbuf, y_scr, m_scr):
    t = pl.program_id(1)
    ts = x_ref.shape[1]
    d = x_ref.shape[2]
    n_lane_groups = abuf.shape[0]
    mix = n_lane_groups * LANES
    conv_k = dw_ref.shape[0]
    sc_k = sc_ref.shape[0]
    n_branch = wbr_ref.shape[0]
    col_a, col_b, col_c, col_d = 0, 2 * mix, 3 * mix, 6 * mix
    col_g = 8 * mix

    h_scr[...] = _rms(x_ref[0], nrm_ref[...]).astype(BF16)

    _carry_halo(t, abuf, HALO_A, ts)
    _carry_halo(t, pbuf, HALO_P, ts)
    _carry_halo(t, cbuf, HALO_C, ts)

    pa = _dot(h_scr[...], win_ref[:, col_a:col_a + 2 * mix])
    _store_lane_groups(abuf, HALO_A, pa[:, :mix] * _sigmoid(pa[:, mix:]))

    for c in range(ts // CONV_ROWS):
        base = c * CONV_ROWS
        accs = []
        for j in range(n_lane_groups):
            ls = slice(j * LANES, (j + 1) * LANES)
            acc = jnp.broadcast_to(dwb_ref[:, ls], (CONV_ROWS, LANES))
            for k in range(conv_k):
                off = HALO_A - (conv_k - 1) + k
                acc = acc + dw_ref[k:k + 1, ls] * abuf[j, pl.ds(base + off, CONV_ROWS), :]
            accs.append(acc)
        y = _layer_norm(jnp.concatenate(accs, axis=-1), lag_ref[...], lab_ref[...])
        y_scr[0, pl.ds(base, CONV_ROWS), :] = (y * _sigmoid(y)).astype(BF16)

    _store_lane_groups(pbuf, HALO_P, _dot(h_scr[...], win_ref[:, col_b:col_b + mix]))
    pos = (t * ts + 1 + lax.broadcasted_iota(jnp.int32, (ts, 1), 0)).astype(F32)
    for gi, w in enumerate(POOL_WINDOWS):
        sl = slice(gi * LANES, (gi + 1) * LANES)
        u = pbuf[gi, HALO_P:HALO_P + ts, :]
        s = u
        for j in range(1, w):
            s = s + pbuf[gi, HALO_P - j:HALO_P - j + ts, :]
        pooled = s / jnp.minimum(pos, float(w)) - u
        mixed = _dot(pooled.astype(BF16), pw_ref[gi]) * ps_ref[:, sl]
        y_scr[1, :, sl] = mixed.astype(BF16)

    pc = _dot(h_scr[...], win_ref[:, col_c:col_c + 3 * mix])
    _store_lane_groups(cbuf, HALO_C, pc[:, mix:2 * mix] * pc[:, 2 * mix:])
    for j in range(n_lane_groups):
        ls = slice(j * LANES, (j + 1) * LANES)
        conv = None
        for k in range(sc_k):
            off = HALO_C - (sc_k - 1) + k
            term = sc_ref[k:k + 1, ls] * cbuf[j, off:off + ts, :]
            conv = term if conv is None else conv + term
        y_scr[2, :, ls] = (pc[:, ls] * conv).astype(BF16)

    pd = _dot(h_scr[...], win_ref[:, col_d:col_d + 2 * mix])
    vn = _layer_norm(pd[:, mix:], ldg_ref[...], ldb_ref[...]).astype(BF16)
    u = pd[:, :mix]
    tri = (lax.broadcasted_iota(jnp.int32, (GMLP_CHUNK, GMLP_CHUNK), 0)
           >= lax.broadcasted_iota(jnp.int32, (GMLP_CHUNK, GMLP_CHUNK), 1))
    n_groups = ws_ref.shape[0]
    gg = mix // n_groups
    for g in range(n_groups):
        wsm = jnp.where(tri, ws_ref[g], 0.0).astype(BF16)
        bcol = bst_ref[:, g:g + 1]
        cs = slice(g * gg, (g + 1) * gg)
        for c in range(ts // GMLP_CHUNK):
            rs = slice(c * GMLP_CHUNK, (c + 1) * GMLP_CHUNK)
            sg = _dot(wsm, vn[rs, cs]) + bcol
            y_scr[3, rs, cs] = (u[rs, cs] * sg).astype(BF16)

    for n in range(d // MERGE_COLS):
        ns = slice(n * MERGE_COLS, (n + 1) * MERGE_COLS)
        merged = None
        for k in range(n_branch):
            gc = col_g + k * d + n * MERGE_COLS
            gate = _sigmoid(_dot(h_scr[...], win_ref[:, gc:gc + MERGE_COLS]))
            term = gate * _dot(y_scr[k], wbr_ref[k, :, ns])
            merged = term if merged is None else merged + term
        m_scr[:, ns] = merged.astype(BF16)
    o_ref[0] = x_ref[0] + _dot(m_scr[...], wout_ref[...])


def _ffn_kernel(x_ref, p_ref, nm_ref, wup_ref, wdn_ref, npl_ref, wple_ref, wpg_ref, nf_ref, o_ref,
                h_scr, u_scr, *, final_norm):
    d_ff = wup_ref.shape[1]
    x = x_ref[...]
    h_scr[...] = _rms(x, nm_ref[...]).astype(BF16)
    for f in range(d_ff // FF_COLS):
        fs = slice(f * FF_COLS, (f + 1) * FF_COLS)
        up = jnp.maximum(_dot(h_scr[...], wup_ref[:, fs]), 0.0)
        u_scr[:, fs] = (up * up).astype(BF16)
    x = x + _dot(u_scr[...], wdn_ref[...])
    h2 = _rms(x, npl_ref[...]).astype(BF16)
    gate = _sigmoid(_dot(h2, wpg_ref[...]))
    x = x + _dot(p_ref[...].astype(BF16), wple_ref[...]) * gate
    if final_norm:
        x = _rms(x, nf_ref[...])
    o_ref[...] = x


def _resident(shape, layer):
    nd = len(shape)
    return pl.BlockSpec((None,) + tuple(shape[1:]), lambda *_: (layer,) + (0,) * (nd - 1),
                        pipeline_mode=pl.Buffered(1))


def _mixer_call(layer, x, norm_mix, w_in, conf_dw, conf_dw_b, conf_ln_g, conf_ln_b, pool_w,
                pool_scale, sc_conv, gmlp_ln_g, gmlp_ln_b, gmlp_ws, gmlp_bs_t, w_branch, w_out):
    b, s, d = x.shape
    mix = conf_dw.shape[-1]
    ts = TIME_TILE
    assert s % ts == 0 and ts % GMLP_CHUNK == 0 and ts % CONV_ROWS == 0
    assert conf_dw.shape[1] - 1 <= HALO_A and max(POOL_WINDOWS) - 1 <= HALO_P
    assert sc_conv.shape[1] - 1 <= HALO_C and d % MERGE_COLS == 0
    assert mix == len(POOL_WINDOWS) * LANES and pool_w.shape[-1] == LANES
    params = (norm_mix, w_in, conf_dw, conf_dw_b, conf_ln_g, conf_ln_b, pool_w, pool_scale,
              sc_conv, gmlp_ln_g, gmlp_ln_b, gmlp_ws, gmlp_bs_t, w_branch, w_out)
    x_spec = pl.BlockSpec((1, ts, d), lambda bi, ti: (bi, ti, 0))
    return pl.pallas_call(
        _mixer_kernel,
        grid=(b, s // ts),
        in_specs=[x_spec] + [_resident(a.shape, layer) for a in params],
        out_specs=x_spec,
        out_shape=jax.ShapeDtypeStruct(x.shape, x.dtype),
        scratch_shapes=[
            pltpu.VMEM((ts, d), BF16),
            pltpu.VMEM((mix // LANES, HALO_A + ts, LANES), F32),
            pltpu.VMEM((mix // LANES, HALO_P + ts, LANES), F32),
            pltpu.VMEM((mix // LANES, HALO_C + ts, LANES), F32),
            pltpu.VMEM((w_branch.shape[1], ts, mix), BF16),
            pltpu.VMEM((ts, d), BF16),
        ],
        name=f"mixer_l{layer}",
        compiler_params=pltpu.CompilerParams(
            dimension_semantics=("arbitrary", "arbitrary"),
            vmem_limit_bytes=VMEM_LIMIT_BYTES),
    )(x, *params)


def _ffn_call(layer, final_norm, x2, p, norm_mlp, w_up, w_down, norm_ple, w_ple, w_ple_gate,
              norm_final):
    m, d = x2.shape
    tm = TOKEN_TILE
    ple = p.shape[-1]
    assert m % tm == 0 and w_up.shape[-1] % FF_COLS == 0
    params = (norm_mlp, w_up, w_down, norm_ple, w_ple, w_ple_gate)
    x_spec = pl.BlockSpec((tm, d), lambda i: (i, 0))
    return pl.pallas_call(
        functools.partial(_ffn_kernel, final_norm=final_norm),
        grid=(m // tm,),
        in_specs=[x_spec, pl.BlockSpec((None, tm, ple), lambda i: (layer, i, 0))]
        + [_resident(a.shape, layer) for a in params]
        + [pl.BlockSpec(norm_final.shape, lambda i: (0, 0), pipeline_mode=pl.Buffered(1))],
        out_specs=x_spec,
        out_shape=jax.ShapeDtypeStruct(x2.shape, x2.dtype),
        scratch_shapes=[
            pltpu.VMEM((tm, d), BF16),
            pltpu.VMEM((tm, w_up.shape[-1]), BF16),
        ],
        name=f"ffn_l{layer}",
        compiler_params=pltpu.CompilerParams(
            dimension_semantics=("arbitrary",),
            vmem_limit_bytes=VMEM_LIMIT_BYTES),
    )(x2, p, *params, norm_final)


def kernel(x, p, norm_mix, w_in, conf_dw, conf_dw_b, conf_ln_g, conf_ln_b, pool_w, pool_scale, sc_conv, gmlp_ln_g, gmlp_ln_b, gmlp_ws, gmlp_bs, w_branch, w_out, norm_mlp, w_up, w_down, norm_ple, w_ple, w_ple_gate, norm_final):
    depth = w_in.shape[0]
    b, s, d = x.shape
    row = lambda a: a[:, None, :]
    mixer_params = (
        row(norm_mix), w_in.astype(BF16), conf_dw, row(conf_dw_b), row(conf_ln_g), row(conf_ln_b),
        pool_w.astype(BF16), row(pool_scale), sc_conv, row(gmlp_ln_g), row(gmlp_ln_b), gmlp_ws,
        jnp.swapaxes(gmlp_bs, 1, 2), w_branch.astype(BF16), w_out.astype(BF16))
    ffn_params = (row(norm_mlp), w_up.astype(BF16), w_down.astype(BF16), row(norm_ple),
                  w_ple.astype(BF16), w_ple_gate.astype(BF16))
    p2 = p.reshape(depth, b * s, p.shape[-1])
    nf = norm_final[None, :]
    for i in range(depth):
        x = _mixer_call(i, x, *mixer_params)
        x = _ffn_call(i, i == depth - 1, x.reshape(b * s, d), p2, *ffn_params, nf).reshape(b, s, d)
    return x
```

```python
import functools

import jax
import jax.numpy as jnp
from jax import lax
from jax.experimental import pallas as pl
from jax.experimental.pallas import tpu as pltpu

EPS = 1e-6
POOL_WINDOWS = (2, 4, 8, 16)
GMLP_CHUNK = 128
LANES = 128
BF16_SUBLANES = 16

VMEM_LIMIT_BYTES = 60 * 1024 * 1024

TIME_TILE = 512
TOKEN_TILE = 512
CONV_ROWS = 32
MERGE_COLS = 256
FF_COLS = 512
HALO_A = 32
HALO_P = 16
HALO_C = 8

BF16 = jnp.bfloat16
F32 = jnp.float32


def _dot(a, b):
    return jnp.dot(a, b, preferred_element_type=F32)


def _sigmoid(x):
    return 0.5 * jnp.tanh(0.5 * x) + 0.5


def _rms(x, g):
    ms = jnp.mean(x * x, axis=-1, keepdims=True)
    return x * lax.rsqrt(ms + EPS) * g


def _layer_norm(x, g, b):
    mu = jnp.mean(x, axis=-1, keepdims=True)
    xc = x - mu
    var = jnp.mean(xc * xc, axis=-1, keepdims=True)
    return xc * lax.rsqrt(var + EPS) * g + b


def _cast_slabs(src_refs, dst_refs):
    for src, dst in zip(src_refs, dst_refs, strict=True):
        dst[...] = src[...].astype(BF16)


def _carry_halo(t, buf, halo, rows):
    @pl.when(t == 0)
    def _():
        buf[:, 0:halo, :] = jnp.zeros((buf.shape[0], halo, buf.shape[2]), buf.dtype)

    @pl.when(t > 0)
    def _():
        buf[:, 0:halo, :] = buf[:, rows:rows + halo, :]


def _store_lane_groups(buf, halo, val):
    rows = val.shape[0]
    for j in range(buf.shape[0]):
        buf[j, halo:halo + rows, :] = val[:, j * LANES:(j + 1) * LANES]


def _mixer_kernel(x_ref, xn_ref, nrm_ref, win_ref, dw_ref, dwb_ref, lag_ref, lab_ref, pw_ref,
                  ps_ref, sc_ref, ldg_ref, ldb_ref, ws_ref, bst_ref, wbr_ref, wout_ref,
                  c0_ref, c1_ref, c2_ref, o_ref, d0_ref, d1_ref, d2_ref,
                  h_scr, abuf, pbuf, cbuf, y_scr, g0_scr, part_scr, m_scr):
    t = pl.program_id(1)
    step = pl.program_id(0) * pl.num_programs(1) + t
    ts = x_ref.shape[1]
    d = x_ref.shape[2]
    n_lane_groups = abuf.shape[0]
    mix = n_lane_groups * LANES
    conv_k = dw_ref.shape[0]
    sc_k = sc_ref.shape[0]
    n_branch = wbr_ref.shape[0]
    col_a, col_b, col_c, col_d = 0, 2 * mix, 3 * mix, 6 * mix
    col_g = 8 * mix

    _carry_halo(t, abuf, HALO_A, ts)
    _carry_halo(t, pbuf, HALO_P, ts)
    _carry_halo(t, cbuf, HALO_C, ts)

    @pl.when(step == 0)
    def _():
        h_scr[0] = _rms(x_ref[0], nrm_ref[...]).astype(BF16)

    slot = step % 2
    _cast_slabs((c0_ref, c1_ref, c2_ref), (d0_ref, d1_ref, d2_ref))

    pa = _dot(h_scr[slot], win_ref[:, col_a:col_a + 2 * mix])
    _store_lane_groups(abuf, HALO_A, pa[:, :mix] * _sigmoid(pa[:, mix:]))

    for c in range(ts // CONV_ROWS):
        base = c * CONV_ROWS
        accs = []
        for j in range(n_lane_groups):
            ls = slice(j * LANES, (j + 1) * LANES)
            acc = jnp.broadcast_to(dwb_ref[:, ls], (CONV_ROWS, LANES))
            for k in range(conv_k):
                off = HALO_A - (conv_k - 1) + k
                acc = acc + dw_ref[k:k + 1, ls] * abuf[j, pl.ds(base + off, CONV_ROWS), :]
            accs.append(acc)
        y = _layer_norm(jnp.concatenate(accs, axis=-1), lag_ref[...], lab_ref[...])
        y_scr[0, pl.ds(base, CONV_ROWS), :] = (y * _sigmoid(y)).astype(BF16)

    _store_lane_groups(pbuf, HALO_P, _dot(h_scr[slot], win_ref[:, col_b:col_b + mix]))
    pos = (t * ts + 1 + lax.broadcasted_iota(jnp.int32, (ts, 1), 0)).astype(F32)
    for gi, w in enumerate(POOL_WINDOWS):
        sl = slice(gi * LANES, (gi + 1) * LANES)
        u = pbuf[gi, HALO_P:HALO_P + ts, :]
        s = u
        for j in range(1, w):
            s = s + pbuf[gi, HALO_P - j:HALO_P - j + ts, :]
        pooled = s / jnp.minimum(pos, float(w)) - u
        mixed = _dot(pooled.astype(BF16), pw_ref[gi]) * ps_ref[:, sl]
        y_scr[1, :, sl] = mixed.astype(BF16)

    pc = _dot(h_scr[slot], win_ref[:, col_c:col_c + 3 * mix])
    _store_lane_groups(cbuf, HALO_C, pc[:, mix:2 * mix] * pc[:, 2 * mix:])
    for j in range(n_lane_groups):
        ls = slice(j * LANES, (j + 1) * LANES)
        conv = None
        for k in range(sc_k):
            off = HALO_C - (sc_k - 1) + k
            term = sc_ref[k:k + 1, ls] * cbuf[j, off:off + ts, :]
            conv = term if conv is None else conv + term
        y_scr[2, :, ls] = (pc[:, ls] * conv).astype(BF16)

    pd = _dot(h_scr[slot], win_ref[:, col_d:col_d + 2 * mix])
    vn = _layer_norm(pd[:, mix:], ldg_ref[...], ldb_ref[...]).astype(BF16)
    u = pd[:, :mix]
    tri = (lax.broadcasted_iota(jnp.int32, (GMLP_CHUNK, GMLP_CHUNK), 0)
           >= lax.broadcasted_iota(jnp.int32, (GMLP_CHUNK, GMLP_CHUNK), 1))
    n_groups = ws_ref.shape[0]
    gg = mix // n_groups
    for g in range(n_groups):
        wsm = jnp.where(tri, ws_ref[g], 0.0).astype(BF16)
        bcol = bst_ref[:, g:g + 1]
        cs = slice(g * gg, (g + 1) * gg)
        for c in range(ts // GMLP_CHUNK):
            rs = slice(c * GMLP_CHUNK, (c + 1) * GMLP_CHUNK)
            sg = _dot(wsm, vn[rs, cs]) + bcol
            y_scr[3, rs, cs] = (u[rs, cs] * sg).astype(BF16)

    def gate(k, n):
        gc = col_g + k * d + n * MERGE_COLS
        return _sigmoid(_dot(h_scr[slot], win_ref[:, gc:gc + MERGE_COLS]))

    for n in range(d // MERGE_COLS):
        ns = slice(n * MERGE_COLS, (n + 1) * MERGE_COLS)
        g0_scr[:, ns] = gate(0, n)
        part = None
        for k in range(1, n_branch):
            term = gate(k, n) * _dot(y_scr[k], wbr_ref[k, :, ns])
            part = term if part is None else part + term
        part_scr[:, ns] = part

    h_scr[1 - slot] = _rms(xn_ref[0], nrm_ref[...]).astype(BF16)

    for n in range(d // MERGE_COLS):
        ns = slice(n * MERGE_COLS, (n + 1) * MERGE_COLS)
        merged = part_scr[:, ns] + g0_scr[:, ns] * _dot(y_scr[0], wbr_ref[0, :, ns])
        m_scr[:, ns] = merged.astype(BF16)
    o_ref[0] = x_ref[0] + _dot(m_scr[...], wout_ref[...])


def _ffn_kernel(*refs, final_norm, n_cast):
    (x_ref, p_ref, nm_ref, wup_ref, wdn_ref, npl_ref, wple_ref, wpg_ref, nf_ref) = refs[:9]
    cast_src = refs[9:9 + n_cast]
    o_ref = refs[9 + n_cast]
    cast_dst = refs[10 + n_cast:10 + 2 * n_cast]
    h_scr, u_scr = refs[10 + 2 * n_cast:]
    d_ff = wup_ref.shape[1]
    _cast_slabs(cast_src, cast_dst)
    x = x_ref[...]
    h_scr[...] = _rms(x, nm_ref[...]).astype(BF16)
    for f in range(d_ff // FF_COLS):
        fs = slice(f * FF_COLS, (f + 1) * FF_COLS)
        up = jnp.maximum(_dot(h_scr[...], wup_ref[:, fs]), 0.0)
        u_scr[:, fs] = (up * up).astype(BF16)
    x = x + _dot(u_scr[...], wdn_ref[...])
    h2 = _rms(x, npl_ref[...]).astype(BF16)
    gate = _sigmoid(_dot(h2, wpg_ref[...]))
    x = x + _dot(p_ref[...].astype(BF16), wple_ref[...]) * gate
    if final_norm:
        x = _rms(x, nf_ref[...])
    o_ref[...] = x


def _resident(a, layer=None):
    if layer is None:
        return pl.BlockSpec(a.shape, lambda *_: (0,) * a.ndim, pipeline_mode=pl.Buffered(1))
    return pl.BlockSpec((None,) + tuple(a.shape[1:]), lambda *_: (layer,) + (0,) * (a.ndim - 1),
                        pipeline_mode=pl.Buffered(1))


def _slab_specs(stacked, layer, n_steps, step_of):
    shape = stacked.shape[1:]
    rows = shape[-2] // n_steps
    assert rows * n_steps == shape[-2] and rows % BF16_SUBLANES == 0, (shape, n_steps)
    block = shape[:-2] + (rows, shape[-1])
    lead = (0,) * (len(shape) - 2)
    in_spec = pl.BlockSpec((None,) + block, lambda *g: (layer,) + lead + (step_of(*g), 0))
    out_spec = pl.BlockSpec(block, lambda *g: lead + (step_of(*g), 0))
    return in_spec, out_spec, jax.ShapeDtypeStruct(shape, BF16)


def _mixer_call(layer, x, small, w_in, w_branch, w_out, pool_w, cast_next):
    (norm_mix, conf_dw, conf_dw_b, conf_ln_g, conf_ln_b, pool_scale, sc_conv, gmlp_ln_g, gmlp_ln_b,
     gmlp_ws, gmlp_bs_t) = small
    b, s, d = x.shape
    mix = conf_dw.shape[-1]
    ts = TIME_TILE
    n_t = s // ts
    assert s % ts == 0 and ts % GMLP_CHUNK == 0 and ts % CONV_ROWS == 0
    assert conf_dw.shape[1] - 1 <= HALO_A and max(POOL_WINDOWS) - 1 <= HALO_P
    assert sc_conv.shape[1] - 1 <= HALO_C and d % MERGE_COLS == 0
    assert mix == len(POOL_WINDOWS) * LANES and pool_w.shape[-1] == LANES
    x_spec = pl.BlockSpec((1, ts, d), lambda bi, ti: (bi, ti, 0))

    def next_tile(bi, ti):
        n = jnp.minimum(bi * n_t + ti + 1, b * n_t - 1)
        return (n // n_t, n % n_t, 0)

    L = layer
    in_specs = [
        x_spec, pl.BlockSpec((1, ts, d), next_tile), _resident(norm_mix, L), _resident(w_in),
        _resident(conf_dw, L), _resident(conf_dw_b, L), _resident(conf_ln_g, L),
        _resident(conf_ln_b, L), _resident(pool_w, L), _resident(pool_scale, L),
        _resident(sc_conv, L), _resident(gmlp_ln_g, L), _resident(gmlp_ln_b, L),
        _resident(gmlp_ws, L), _resident(gmlp_bs_t, L), _resident(w_branch), _resident(w_out)]
    slabs = [_slab_specs(w, L, b * n_t, lambda bi, ti: bi * n_t + ti) for w in cast_next]
    outs = pl.pallas_call(
        _mixer_kernel,
        grid=(b, n_t),
        in_specs=in_specs + [sl[0] for sl in slabs],
        out_specs=[x_spec] + [sl[1] for sl in slabs],
        out_shape=[jax.ShapeDtypeStruct(x.shape, x.dtype)] + [sl[2] for sl in slabs],
        scratch_shapes=[
            pltpu.VMEM((2, ts, d), BF16),
            pltpu.VMEM((mix // LANES, HALO_A + ts, LANES), F32),
            pltpu.VMEM((mix // LANES, HALO_P + ts, LANES), F32),
            pltpu.VMEM((mix // LANES, HALO_C + ts, LANES), F32),
            pltpu.VMEM((w_branch.shape[0], ts, mix), BF16),
            pltpu.VMEM((ts, d), F32),
            pltpu.VMEM((ts, d), F32),
            pltpu.VMEM((ts, d), BF16),
        ],
        name=f"mixer_l{layer}",
        compiler_params=pltpu.CompilerParams(
            dimension_semantics=("arbitrary", "arbitrary"),
            vmem_limit_bytes=VMEM_LIMIT_BYTES),
    )(x, x, norm_mix, w_in, conf_dw, conf_dw_b, conf_ln_g, conf_ln_b, pool_w, pool_scale, sc_conv,
      gmlp_ln_g, gmlp_ln_b, gmlp_ws, gmlp_bs_t, w_branch, w_out, *cast_next)
    return outs[0], tuple(outs[1:])


def _ffn_call(layer, final_norm, x2, p, small, w_up, w_down, w_ple_gate, w_ple, norm_final,
              cast_next):
    norm_mlp, norm_ple = small
    m, d = x2.shape
    tm = TOKEN_TILE
    n_m = m // tm
    ple = p.shape[-1]
    assert m % tm == 0 and w_up.shape[-1] % FF_COLS == 0
    L = layer
    x_spec = pl.BlockSpec((tm, d), lambda i: (i, 0))
    slabs = [_slab_specs(w, L + 1, n_m, lambda i: i) for w in cast_next]
    outs = pl.pallas_call(
        functools.partial(_ffn_kernel, final_norm=final_norm, n_cast=len(cast_next)),
        grid=(n_m,),
        in_specs=[x_spec, pl.BlockSpec((None, tm, ple), lambda i: (L, i, 0)),
                  _resident(norm_mlp, L), _resident(w_up), _resident(w_down),
                  _resident(norm_ple, L), _resident(w_ple, L), _resident(w_ple_gate),
                  _resident(norm_final)] + [sl[0] for sl in slabs],
        out_specs=[x_spec] + [sl[1] for sl in slabs],
        out_shape=[jax.ShapeDtypeStruct(x2.shape, x2.dtype)] + [sl[2] for sl in slabs],
        scratch_shapes=[
            pltpu.VMEM((tm, d), BF16),
            pltpu.VMEM((tm, w_up.shape[-1]), BF16),
        ],
        name=f"ffn_l{layer}",
        compiler_params=pltpu.CompilerParams(
            dimension_semantics=("arbitrary",),
            vmem_limit_bytes=VMEM_LIMIT_BYTES),
    )(x2, p, norm_mlp, w_up, w_down, norm_ple, w_ple, w_ple_gate, norm_final, *cast_next)
    return outs[0], tuple(outs[1:])


def kernel(x, p, norm_mix, w_in, conf_dw, conf_dw_b, conf_ln_g, conf_ln_b, pool_w, pool_scale, sc_conv, gmlp_ln_g, gmlp_ln_b, gmlp_ws, gmlp_bs, w_branch, w_out, norm_mlp, w_up, w_down, norm_ple, w_ple, w_ple_gate, norm_final):
    depth = w_in.shape[0]
    b, s, d = x.shape
    row = lambda a: a[:, None, :]
    mixer_small = (row(norm_mix), conf_dw, row(conf_dw_b), row(conf_ln_g), row(conf_ln_b),
                   row(pool_scale), sc_conv, row(gmlp_ln_g), row(gmlp_ln_b), gmlp_ws,
                   jnp.swapaxes(gmlp_bs, 1, 2))
    ffn_small = (row(norm_mlp), row(norm_ple))
    pool_w16 = pool_w.astype(BF16)
    w_ple16 = w_ple.astype(BF16)
    p2 = p.reshape(depth, b * s, p.shape[-1])
    nf = norm_final[None, :]
    mixer_w = (w_in[0].astype(BF16), w_branch[0].astype(BF16), w_out[0].astype(BF16))
    for i in range(depth):
        x, ffn_w = _mixer_call(i, x, mixer_small, *mixer_w, pool_w16, (w_up, w_down, w_ple_gate))
        last = i == depth - 1
        x2, mixer_w = _ffn_call(i, last, x.reshape(b * s, d), p2, ffn_small, *ffn_w, w_ple16, nf,
                                () if last else (w_in, w_branch, w_out))
        x = x2.reshape(b, s, d)
    return x
```

```python
import functools

import jax
import jax.numpy as jnp
from jax import lax
from jax.experimental import pallas as pl
from jax.experimental.pallas import tpu as pltpu

EPS = 1e-6
POOL_WINDOWS = (2, 4, 8, 16)
GMLP_CHUNK = 128
LANES = 128
BF16_SUBLANES = 16

VMEM_LIMIT_BYTES = 60 * 1024 * 1024

TIME_TILE = 512
TOKEN_TILE = 1024
CONV_ROWS = 32
MERGE_COLS = 256
FF_COLS = 512
HALO_A = 32
HALO_P = 16
HALO_C = 8

BF16 = jnp.bfloat16
F32 = jnp.float32


def _dot(a, b):
    return jnp.dot(a, b, preferred_element_type=F32)


def _sigmoid(x):
    return 0.5 * jnp.tanh(0.5 * x) + 0.5


def _rms(x, g):
    ms = jnp.mean(x * x, axis=-1, keepdims=True)
    return x * lax.rsqrt(ms + EPS) * g


def _layer_norm(x, g, b):
    mu = jnp.mean(x, axis=-1, keepdims=True)
    xc = x - mu
    var = jnp.mean(xc * xc, axis=-1, keepdims=True)
    return xc * lax.rsqrt(var + EPS) * g + b


def _cast_slabs(src_refs, dst_refs):
    for src, dst in zip(src_refs, dst_refs, strict=True):
        dst[...] = src[...].astype(BF16)


def _carry_halo(t, buf, halo, rows):
    @pl.when(t == 0)
    def _():
        buf[:, 0:halo, :] = jnp.zeros((buf.shape[0], halo, buf.shape[2]), buf.dtype)

    @pl.when(t > 0)
    def _():
        buf[:, 0:halo, :] = buf[:, rows:rows + halo, :]


def _store_lane_groups(buf, halo, val):
    rows = val.shape[0]
    for j in range(buf.shape[0]):
        buf[j, halo:halo + rows, :] = val[:, j * LANES:(j + 1) * LANES]


def _mixer_kernel(x_ref, xn_ref, nrm_ref, win_ref, dw_ref, dwb_ref, lag_ref, lab_ref, pw_ref,
                  ps_ref, sc_ref, ldg_ref, ldb_ref, ws_ref, bst_ref, wbr_ref, wout_ref,
                  c0_ref, c1_ref, c2_ref, o_ref, d0_ref, d1_ref, d2_ref,
                  h_scr, abuf, pbuf, cbuf, y_scr, g0_scr, part_scr, m_scr):
    t = pl.program_id(1)
    step = pl.program_id(0) * pl.num_programs(1) + t
    ts = x_ref.shape[1]
    d = x_ref.shape[2]
    n_lane_groups = abuf.shape[0]
    mix = n_lane_groups * LANES
    conv_k = dw_ref.shape[0]
    sc_k = sc_ref.shape[0]
    n_branch = wbr_ref.shape[0]
    col_a, col_b, col_c, col_d = 0, 2 * mix, 3 * mix, 6 * mix
    col_g = 8 * mix

    _carry_halo(t, abuf, HALO_A, ts)
    _carry_halo(t, pbuf, HALO_P, ts)
    _carry_halo(t, cbuf, HALO_C, ts)

    @pl.when(step == 0)
    def _():
        h_scr[0] = _rms(x_ref[0], nrm_ref[...]).astype(BF16)

    slot = step % 2
    _cast_slabs((c0_ref, c1_ref, c2_ref), (d0_ref, d1_ref, d2_ref))

    pa = _dot(h_scr[slot], win_ref[:, col_a:col_a + 2 * mix])
    _store_lane_groups(abuf, HALO_A, pa[:, :mix] * _sigmoid(pa[:, mix:]))

    for c in range(ts // CONV_ROWS):
        base = c * CONV_ROWS
        accs = []
        for j in range(n_lane_groups):
            ls = slice(j * LANES, (j + 1) * LANES)
            acc = jnp.broadcast_to(dwb_ref[:, ls], (CONV_ROWS, LANES))
            for k in range(conv_k):
                off = HALO_A - (conv_k - 1) + k
                acc = acc + dw_ref[k:k + 1, ls] * abuf[j, pl.ds(base + off, CONV_ROWS), :]
            accs.append(acc)
        y = _layer_norm(jnp.concatenate(accs, axis=-1), lag_ref[...], lab_ref[...])
        y_scr[0, pl.ds(base, CONV_ROWS), :] = (y * _sigmoid(y)).astype(BF16)

    _store_lane_groups(pbuf, HALO_P, _dot(h_scr[slot], win_ref[:, col_b:col_b + mix]))
    pos = (t * ts + 1 + lax.broadcasted_iota(jnp.int32, (ts, 1), 0)).astype(F32)
    for gi, w in enumerate(POOL_WINDOWS):
        sl = slice(gi * LANES, (gi + 1) * LANES)
        u = pbuf[gi, HALO_P:HALO_P + ts, :]
        s = u
        for j in range(1, w):
            s = s + pbuf[gi, HALO_P - j:HALO_P - j + ts, :]
        pooled = s / jnp.minimum(pos, float(w)) - u
        mixed = _dot(pooled.astype(BF16), pw_ref[gi]) * ps_ref[:, sl]
        y_scr[1, :, sl] = mixed.astype(BF16)

    pc = _dot(h_scr[slot], win_ref[:, col_c:col_c + 3 * mix])
    _store_lane_groups(cbuf, HALO_C, pc[:, mix:2 * mix] * pc[:, 2 * mix:])
    for j in range(n_lane_groups):
        ls = slice(j * LANES, (j + 1) * LANES)
        conv = None
        for k in range(sc_k):
            off = HALO_C - (sc_k - 1) + k
            term = sc_ref[k:k + 1, ls] * cbuf[j, off:off + ts, :]
            conv = term if conv is None else conv + term
        y_scr[2, :, ls] = (pc[:, ls] * conv).astype(BF16)

    pd = _dot(h_scr[slot], win_ref[:, col_d:col_d + 2 * mix])
    vn = _layer_norm(pd[:, mix:], ldg_ref[...], ldb_ref[...]).astype(BF16)
    u = pd[:, :mix]
    tri = (lax.broadcasted_iota(jnp.int32, (GMLP_CHUNK, GMLP_CHUNK), 0)
           >= lax.broadcasted_iota(jnp.int32, (GMLP_CHUNK, GMLP_CHUNK), 1))
    n_groups = ws_ref.shape[0]
    gg = mix // n_groups
    for g in range(n_groups):
        wsm = jnp.where(tri, ws_ref[g], 0.0).astype(BF16)
        bcol = bst_ref[:, g:g + 1]
        cs = slice(g * gg, (g + 1) * gg)
        for c in range(ts // GMLP_CHUNK):
            rs = slice(c * GMLP_CHUNK, (c + 1) * GMLP_CHUNK)
            sg = _dot(wsm, vn[rs, cs]) + bcol
            y_scr[3, rs, cs] = (u[rs, cs] * sg).astype(BF16)

    def gate(k, n):
        gc = col_g + k * d + n * MERGE_COLS
        return _sigmoid(_dot(h_scr[slot], win_ref[:, gc:gc + MERGE_COLS]))

    for n in range(d // MERGE_COLS):
        ns = slice(n * MERGE_COLS, (n + 1) * MERGE_COLS)
        g0_scr[:, ns] = gate(0, n)
        part = None
        for k in range(1, n_branch):
            term = gate(k, n) * _dot(y_scr[k], wbr_ref[k, :, ns])
            part = term if part is None else part + term
        part_scr[:, ns] = part

    h_scr[1 - slot] = _rms(xn_ref[0], nrm_ref[...]).astype(BF16)

    for n in range(d // MERGE_COLS):
        ns = slice(n * MERGE_COLS, (n + 1) * MERGE_COLS)
        merged = part_scr[:, ns] + g0_scr[:, ns] * _dot(y_scr[0], wbr_ref[0, :, ns])
        m_scr[:, ns] = merged.astype(BF16)
    o_ref[0] = x_ref[0] + _dot(m_scr[...], wout_ref[...])


def _ffn_kernel(*refs, final_norm, n_cast):
    (x_ref, p_ref, nm_ref, wup_ref, wdn_ref, npl_ref, wple_ref, wpg_ref, nf_ref) = refs[:9]
    cast_src = refs[9:9 + n_cast]
    o_ref = refs[9 + n_cast]
    cast_dst = refs[10 + n_cast:10 + 2 * n_cast]
    h_scr, u_scr = refs[10 + 2 * n_cast:]
    d_ff = wup_ref.shape[1]
    _cast_slabs(cast_src, cast_dst)
    x = x_ref[...]
    h_scr[...] = _rms(x, nm_ref[...]).astype(BF16)
    for f in range(d_ff // FF_COLS):
        fs = slice(f * FF_COLS, (f + 1) * FF_COLS)
        up = jnp.maximum(_dot(h_scr[...], wup_ref[:, fs]), 0.0)
        u_scr[:, fs] = (up * up).astype(BF16)
    x = x + _dot(u_scr[...], wdn_ref[...])
    h2 = _rms(x, npl_ref[...]).astype(BF16)
    gate = _sigmoid(_dot(h2, wpg_ref[...]))
    x = x + _dot(p_ref[...].astype(BF16), wple_ref[...]) * gate
    if final_norm:
        x = _rms(x, nf_ref[...])
    o_ref[...] = x


def _resident(a, layer=None):
    if layer is None:
        return pl.BlockSpec(a.shape, lambda *_: (0,) * a.ndim, pipeline_mode=pl.Buffered(1))
    return pl.BlockSpec((None,) + tuple(a.shape[1:]), lambda *_: (layer,) + (0,) * (a.ndim - 1),
                        pipeline_mode=pl.Buffered(1))


def _slab_specs(stacked, layer, n_steps, step_of):
    shape = stacked.shape[1:]
    rows = shape[-2] // n_steps
    assert rows * n_steps == shape[-2] and rows % BF16_SUBLANES == 0, (shape, n_steps)
    block = shape[:-2] + (rows, shape[-1])
    lead = (0,) * (len(shape) - 2)
    in_spec = pl.BlockSpec((None,) + block, lambda *g: (layer,) + lead + (step_of(*g), 0))
    out_spec = pl.BlockSpec(block, lambda *g: lead + (step_of(*g), 0))
    return in_spec, out_spec, jax.ShapeDtypeStruct(shape, BF16)


def _mixer_call(layer, x, small, w_in, w_branch, w_out, pool_w, cast_next):
    (norm_mix, conf_dw, conf_dw_b, conf_ln_g, conf_ln_b, pool_scale, sc_conv, gmlp_ln_g, gmlp_ln_b,
     gmlp_ws, gmlp_bs_t) = small
    b, s, d = x.shape
    mix = conf_dw.shape[-1]
    ts = TIME_TILE
    n_t = s // ts
    assert s % ts == 0 and ts % GMLP_CHUNK == 0 and ts % CONV_ROWS == 0
    assert conf_dw.shape[1] - 1 <= HALO_A and max(POOL_WINDOWS) - 1 <= HALO_P
    assert sc_conv.shape[1] - 1 <= HALO_C and d % MERGE_COLS == 0
    assert mix == len(POOL_WINDOWS) * LANES and pool_w.shape[-1] == LANES
    x_spec = pl.BlockSpec((1, ts, d), lambda bi, ti: (bi, ti, 0))

    def next_tile(bi, ti):
        n = jnp.minimum(bi * n_t + ti + 1, b * n_t - 1)
        return (n // n_t, n % n_t, 0)

    L = layer
    in_specs = [
        x_spec, pl.BlockSpec((1, ts, d), next_tile), _resident(norm_mix, L), _resident(w_in),
        _resident(conf_dw, L), _resident(conf_dw_b, L), _resident(conf_ln_g, L),
        _resident(conf_ln_b, L), _resident(pool_w, L), _resident(pool_scale, L),
        _resident(sc_conv, L), _resident(gmlp_ln_g, L), _resident(gmlp_ln_b, L),
        _resident(gmlp_ws, L), _resident(gmlp_bs_t, L), _resident(w_branch), _resident(w_out)]
    slabs = [_slab_specs(w, L, b * n_t, lambda bi, ti: bi * n_t + ti) for w in cast_next]
    outs = pl.pallas_call(
        _mixer_kernel,
        grid=(b, n_t),
        in_specs=in_specs + [sl[0] for sl in slabs],
        out_specs=[x_spec] + [sl[1] for sl in slabs],
        out_shape=[jax.ShapeDtypeStruct(x.shape, x.dtype)] + [sl[2] for sl in slabs],
        scratch_shapes=[
            pltpu.VMEM((2, ts, d), BF16),
            pltpu.VMEM((mix // LANES, HALO_A + ts, LANES), F32),
            pltpu.VMEM((mix // LANES, HALO_P + ts, LANES), F32),
            pltpu.VMEM((mix // LANES, HALO_C + ts, LANES), F32),
            pltpu.VMEM((w_branch.shape[0], ts, mix), BF16),
            pltpu.VMEM((ts, d), F32),
            pltpu.VMEM((ts, d), F32),
            pltpu.VMEM((ts, d), BF16),
        ],
        name=f"mixer_l{layer}",
        compiler_params=pltpu.CompilerParams(
            dimension_semantics=("arbitrary", "arbitrary"),
            vmem_limit_bytes=VMEM_LIMIT_BYTES),
    )(x, x, norm_mix, w_in, conf_dw, conf_dw_b, conf_ln_g, conf_ln_b, pool_w, pool_scale, sc_conv,
      gmlp_ln_g, gmlp_ln_b, gmlp_ws, gmlp_bs_t, w_branch, w_out, *cast_next)
    return outs[0], tuple(outs[1:])


def _ffn_call(layer, final_norm, x2, p, small, w_up, w_down, w_ple_gate, w_ple, norm_final,
              cast_next):
    norm_mlp, norm_ple = small
    m, d = x2.shape
    tm = TOKEN_TILE
    n_m = m // tm
    ple = p.shape[-1]
    assert m % tm == 0 and w_up.shape[-1] % FF_COLS == 0
    L = layer
    x_spec = pl.BlockSpec((tm, d), lambda i: (i, 0))
    slabs = [_slab_specs(w, L + 1, n_m, lambda i: i) for w in cast_next]
    outs = pl.pallas_call(
        functools.partial(_ffn_kernel, final_norm=final_norm, n_cast=len(cast_next)),
        grid=(n_m,),
        in_specs=[x_spec, pl.BlockSpec((None, tm, ple), lambda i: (L, i, 0)),
                  _resident(norm_mlp, L), _resident(w_up), _resident(w_down),
                  _resident(norm_ple, L), _resident(w_ple, L), _resident(w_ple_gate),
                  _resident(norm_final)] + [sl[0] for sl in slabs],
        out_specs=[x_spec] + [sl[1] for sl in slabs],
        out_shape=[jax.ShapeDtypeStruct(x2.shape, x2.dtype)] + [sl[2] for sl in slabs],
        scratch_shapes=[
            pltpu.VMEM((tm, d), BF16),
            pltpu.VMEM((tm, w_up.shape[-1]), BF16),
        ],
        name=f"ffn_l{layer}",
        compiler_params=pltpu.CompilerParams(
            dimension_semantics=("arbitrary",),
            vmem_limit_bytes=VMEM_LIMIT_BYTES),
    )(x2, p, norm_mlp, w_up, w_down, norm_ple, w_ple, w_ple_gate, norm_final, *cast_next)
    return outs[0], tuple(outs[1:])


def kernel(x, p, norm_mix, w_in, conf_dw, conf_dw_b, conf_ln_g, conf_ln_b, pool_w, pool_scale, sc_conv, gmlp_ln_g, gmlp_ln_b, gmlp_ws, gmlp_bs, w_branch, w_out, norm_mlp, w_up, w_down, norm_ple, w_ple, w_ple_gate, norm_final):
    depth = w_in.shape[0]
    b, s, d = x.shape
    row = lambda a: a[:, None, :]
    mixer_small = (row(norm_mix), conf_dw, row(conf_dw_b), row(conf_ln_g), row(conf_ln_b),
                   row(pool_scale), sc_conv, row(gmlp_ln_g), row(gmlp_ln_b), gmlp_ws,
                   jnp.swapaxes(gmlp_bs, 1, 2))
    ffn_small = (row(norm_mlp), row(norm_ple))
    pool_w16 = pool_w.astype(BF16)
    w_ple16 = w_ple.astype(BF16)
    p2 = p.reshape(depth, b * s, p.shape[-1])
    nf = norm_final[None, :]
    mixer_w = (w_in[0].astype(BF16), w_branch[0].astype(BF16), w_out[0].astype(BF16))
    for i in range(depth):
        x, ffn_w = _mixer_call(i, x, mixer_small, *mixer_w, pool_w16, (w_up, w_down, w_ple_gate))
        last = i == depth - 1
        x2, mixer_w = _ffn_call(i, last, x.reshape(b * s, d), p2, ffn_small, *ffn_w, w_ple16, nf,
                                () if last else (w_in, w_branch, w_out))
        x = x2.reshape(b, s, d)
    return x
```

```python
import functools

import jax
import jax.numpy as jnp
from jax import lax
from jax.experimental import pallas as pl
from jax.experimental.pallas import tpu as pltpu

EPS = 1e-6
POOL_WINDOWS = (2, 4, 8, 16)
GMLP_CHUNK = 128
LANES = 128
BF16_SUBLANES = 16

VMEM_LIMIT_BYTES = 60 * 1024 * 1024

TIME_TILE = 512
TOKEN_TILE = 1024
CONV_ROWS = 32
MERGE_COLS = 256
FF_COLS = 512
OUT_COLS = 256
HALO_A = 32
HALO_P = 16
HALO_C = 8

BF16 = jnp.bfloat16
F32 = jnp.float32


def _dot(a, b):
    return jnp.dot(a, b, preferred_element_type=F32)


def _sigmoid(x):
    return 0.5 * jnp.tanh(0.5 * x) + 0.5


def _rms(x, g):
    ms = jnp.mean(x * x, axis=-1, keepdims=True)
    return x * lax.rsqrt(ms + EPS) * g


def _layer_norm(x, g, b):
    mu = jnp.mean(x, axis=-1, keepdims=True)
    xc = x - mu
    var = jnp.mean(xc * xc, axis=-1, keepdims=True)
    return xc * lax.rsqrt(var + EPS) * g + b


def _cast_slabs(src_refs, dst_refs):
    for src, dst in zip(src_refs, dst_refs, strict=True):
        dst[...] = src[...].astype(BF16)


def _carry_halo(t, buf, halo, rows):
    @pl.when(t == 0)
    def _():
        buf[:, 0:halo, :] = jnp.zeros((buf.shape[0], halo, buf.shape[2]), buf.dtype)

    @pl.when(t > 0)
    def _():
        buf[:, 0:halo, :] = buf[:, rows:rows + halo, :]


def _store_lane_groups(buf, halo, val):
    rows = val.shape[0]
    for j in range(buf.shape[0]):
        buf[j, halo:halo + rows, :] = val[:, j * LANES:(j + 1) * LANES]


def _mixer_kernel(x_ref, xn_ref, vec_ref, win_ref, dw_ref, pw_ref, sc_ref, ws_ref, bst_ref,
                  wbr_ref, wout_ref, c0_ref, c1_ref, c2_ref, o_ref, d0_ref, d1_ref, d2_ref,
                  h_scr, abuf, pbuf, cbuf, y_scr, g0_scr, part_scr, m_scr):
    t = pl.program_id(1)
    step = pl.program_id(0) * pl.num_programs(1) + t
    ts = x_ref.shape[1]
    d = x_ref.shape[2]
    n_lane_groups = abuf.shape[0]
    mix = n_lane_groups * LANES
    nrm_ref = vec_ref.at[:, 0:d]
    dwb_ref, lag_ref, lab_ref, ps_ref, ldg_ref, ldb_ref = (
        vec_ref.at[:, d + i * mix:d + (i + 1) * mix] for i in range(6))
    conv_k = dw_ref.shape[0]
    sc_k = sc_ref.shape[0]
    n_branch = wbr_ref.shape[0]
    col_a, col_b, col_c, col_d = 0, 2 * mix, 3 * mix, 6 * mix
    col_g = 8 * mix

    _carry_halo(t, abuf, HALO_A, ts)
    _carry_halo(t, pbuf, HALO_P, ts)
    _carry_halo(t, cbuf, HALO_C, ts)

    @pl.when(step == 0)
    def _():
        h_scr[0] = _rms(x_ref[0], nrm_ref[...]).astype(BF16)

    slot = step % 2
    _cast_slabs((c0_ref, c1_ref, c2_ref), (d0_ref, d1_ref, d2_ref))

    pa = _dot(h_scr[slot], win_ref[:, col_a:col_a + 2 * mix])
    _store_lane_groups(abuf, HALO_A, pa[:, :mix] * _sigmoid(pa[:, mix:]))

    for c in range(ts // CONV_ROWS):
        base = c * CONV_ROWS
        accs = []
        for j in range(n_lane_groups):
            ls = slice(j * LANES, (j + 1) * LANES)
            acc = jnp.broadcast_to(dwb_ref[:, ls], (CONV_ROWS, LANES))
            for k in range(conv_k):
                off = HALO_A - (conv_k - 1) + k
                acc = acc + dw_ref[k:k + 1, ls] * abuf[j, pl.ds(base + off, CONV_ROWS), :]
            accs.append(acc)
        y = _layer_norm(jnp.concatenate(accs, axis=-1), lag_ref[...], lab_ref[...])
        y_scr[0, pl.ds(base, CONV_ROWS), :] = (y * _sigmoid(y)).astype(BF16)

    _store_lane_groups(pbuf, HALO_P, _dot(h_scr[slot], win_ref[:, col_b:col_b + mix]))
    pos = (t * ts + 1 + lax.broadcasted_iota(jnp.int32, (ts, 1), 0)).astype(F32)
    for gi, w in enumerate(POOL_WINDOWS):
        sl = slice(gi * LANES, (gi + 1) * LANES)
        u = pbuf[gi, HALO_P:HALO_P + ts, :]
        s = u
        for j in range(1, w):
            s = s + pbuf[gi, HALO_P - j:HALO_P - j + ts, :]
        pooled = s / jnp.minimum(pos, float(w)) - u
        mixed = _dot(pooled.astype(BF16), pw_ref[gi]) * ps_ref[:, sl]
        y_scr[1, :, sl] = mixed.astype(BF16)

    pc = _dot(h_scr[slot], win_ref[:, col_c:col_c + 3 * mix])
    _store_lane_groups(cbuf, HALO_C, pc[:, mix:2 * mix] * pc[:, 2 * mix:])
    for j in range(n_lane_groups):
        ls = slice(j * LANES, (j + 1) * LANES)
        conv = None
        for k in range(sc_k):
            off = HALO_C - (sc_k - 1) + k
            term = sc_ref[k:k + 1, ls] * cbuf[j, off:off + ts, :]
            conv = term if conv is None else conv + term
        y_scr[2, :, ls] = (pc[:, ls] * conv).astype(BF16)

    pd = _dot(h_scr[slot], win_ref[:, col_d:col_d + 2 * mix])
    vn = _layer_norm(pd[:, mix:], ldg_ref[...], ldb_ref[...]).astype(BF16)
    u = pd[:, :mix]
    tri = (lax.broadcasted_iota(jnp.int32, (GMLP_CHUNK, GMLP_CHUNK), 0)
           >= lax.broadcasted_iota(jnp.int32, (GMLP_CHUNK, GMLP_CHUNK), 1))
    n_groups = ws_ref.shape[0]
    gg = mix // n_groups
    for g in range(n_groups):
        wsm = jnp.where(tri, ws_ref[g], 0.0).astype(BF16)
        bcol = bst_ref[:, g:g + 1]
        cs = slice(g * gg, (g + 1) * gg)
        for c in range(ts // GMLP_CHUNK):
            rs = slice(c * GMLP_CHUNK, (c + 1) * GMLP_CHUNK)
            sg = _dot(wsm, vn[rs, cs]) + bcol
            y_scr[3, rs, cs] = (u[rs, cs] * sg).astype(BF16)

    def gate(k, n):
        gc = col_g + k * d + n * MERGE_COLS
        return _sigmoid(_dot(h_scr[slot], win_ref[:, gc:gc + MERGE_COLS]))

    for n in range(d // MERGE_COLS):
        ns = slice(n * MERGE_COLS, (n + 1) * MERGE_COLS)
        g0_scr[:, ns] = gate(0, n)
        part = None
        for k in range(1, n_branch):
            term = gate(k, n) * _dot(y_scr[k], wbr_ref[k, :, ns])
            part = term if part is None else part + term
        part_scr[:, ns] = part

    h_scr[1 - slot] = _rms(xn_ref[0], nrm_ref[...]).astype(BF16)

    for n in range(d // MERGE_COLS):
        ns = slice(n * MERGE_COLS, (n + 1) * MERGE_COLS)
        merged = part_scr[:, ns] + g0_scr[:, ns] * _dot(y_scr[0], wbr_ref[0, :, ns])
        m_scr[:, ns] = merged.astype(BF16)
    o_ref[0] = x_ref[0] + _dot(m_scr[...], wout_ref[...])


def _ffn_kernel(*refs, final_norm, n_cast):
    (x_ref, p_ref, nm_ref, wup_ref, wdn_ref, npl_ref, wple_ref, wpg_ref, nf_ref) = refs[:9]
    cast_src = refs[9:9 + n_cast]
    o_ref = refs[9 + n_cast]
    cast_dst = refs[10 + n_cast:10 + 2 * n_cast]
    h_scr, u_scr = refs[10 + 2 * n_cast:]
    d_ff = wup_ref.shape[1]
    _cast_slabs(cast_src, cast_dst)
    emb = _dot(p_ref[...].astype(BF16), wple_ref[...])
    x = x_ref[...]
    h_scr[...] = _rms(x, nm_ref[...]).astype(BF16)
    for f in range(d_ff // FF_COLS):
        fs = slice(f * FF_COLS, (f + 1) * FF_COLS)
        up = jnp.maximum(_dot(h_scr[...], wup_ref[:, fs]), 0.0)
        u_scr[:, fs] = (up * up).astype(BF16)
    x = x + _dot(u_scr[...], wdn_ref[...])
    h_scr[...] = _rms(x, npl_ref[...]).astype(BF16)
    if final_norm:
        gate = _sigmoid(_dot(h_scr[...], wpg_ref[...]))
        o_ref[...] = _rms(x + emb * gate, nf_ref[...])
    else:
        for n in range(x.shape[1] // OUT_COLS):
            ns = slice(n * OUT_COLS, (n + 1) * OUT_COLS)
            gate = _sigmoid(_dot(h_scr[...], wpg_ref[:, ns]))
            o_ref[:, ns] = x[:, ns] + emb[:, ns] * gate


def _resident(a, layer=None):
    if layer is None:
        return pl.BlockSpec(a.shape, lambda *_: (0,) * a.ndim, pipeline_mode=pl.Buffered(1))
    return pl.BlockSpec((None,) + tuple(a.shape[1:]), lambda *_: (layer,) + (0,) * (a.ndim - 1),
                        pipeline_mode=pl.Buffered(1))


def _slab_specs(stacked, layer, n_steps, step_of):
    shape = stacked.shape[1:]
    rows = shape[-2] // n_steps
    assert rows * n_steps == shape[-2] and rows % BF16_SUBLANES == 0, (shape, n_steps)
    block = shape[:-2] + (rows, shape[-1])
    lead = (0,) * (len(shape) - 2)
    in_spec = pl.BlockSpec((None,) + block, lambda *g: (layer,) + lead + (step_of(*g), 0))
    out_spec = pl.BlockSpec(block, lambda *g: lead + (step_of(*g), 0))
    return in_spec, out_spec, jax.ShapeDtypeStruct(shape, BF16)


def _mixer_call(layer, x, small, w_in, w_branch, w_out, pool_w, cast_next):
    vectors, conf_dw, sc_conv, gmlp_ws, gmlp_bs_t = small
    b, s, d = x.shape
    mix = conf_dw.shape[-1]
    assert vectors.shape[-1] == d + 6 * mix
    ts = TIME_TILE
    n_t = s // ts
    assert s % ts == 0 and ts % GMLP_CHUNK == 0 and ts % CONV_ROWS == 0
    assert conf_dw.shape[1] - 1 <= HALO_A and max(POOL_WINDOWS) - 1 <= HALO_P
    assert sc_conv.shape[1] - 1 <= HALO_C and d % MERGE_COLS == 0
    assert mix == len(POOL_WINDOWS) * LANES and pool_w.shape[-1] == LANES
    x_spec = pl.BlockSpec((1, ts, d), lambda bi, ti: (bi, ti, 0))

    def next_tile(bi, ti):
        n = jnp.minimum(bi * n_t + ti + 1, b * n_t - 1)
        return (n // n_t, n % n_t, 0)

    L = layer
    in_specs = [
        x_spec, pl.BlockSpec((1, ts, d), next_tile), _resident(vectors, L), _resident(w_in),
        _resident(conf_dw, L), _resident(pool_w, L), _resident(sc_conv, L),
        _resident(gmlp_ws, L), _resident(gmlp_bs_t, L), _resident(w_branch), _resident(w_out)]
    slabs = [_slab_specs(w, L, b * n_t, lambda bi, ti: bi * n_t + ti) for w in cast_next]
    outs = pl.pallas_call(
        _mixer_kernel,
        grid=(b, n_t),
        in_specs=in_specs + [sl[0] for sl in slabs],
        out_specs=[x_spec] + [sl[1] for sl in slabs],
        out_shape=[jax.ShapeDtypeStruct(x.shape, x.dtype)] + [sl[2] for sl in slabs],
        scratch_shapes=[
            pltpu.VMEM((2, ts, d), BF16),
            pltpu.VMEM((mix // LANES, HALO_A + ts, LANES), F32),
            pltpu.VMEM((mix // LANES, HALO_P + ts, LANES), F32),
            pltpu.VMEM((mix // LANES, HALO_C + ts, LANES), F32),
            pltpu.VMEM((w_branch.shape[0], ts, mix), BF16),
            pltpu.VMEM((ts, d), F32),
            pltpu.VMEM((ts, d), F32),
            pltpu.VMEM((ts, d), BF16),
        ],
        name=f"mixer_l{layer}",
        compiler_params=pltpu.CompilerParams(
            dimension_semantics=("arbitrary", "arbitrary"),
            vmem_limit_bytes=VMEM_LIMIT_BYTES),
    )(x, x, vectors, w_in, conf_dw, pool_w, sc_conv, gmlp_ws, gmlp_bs_t, w_branch, w_out,
      *cast_next)
    return outs[0], tuple(outs[1:])


def _ffn_call(layer, final_norm, x2, p, small, w_up, w_down, w_ple_gate, w_ple, norm_final,
              cast_next):
    norm_mlp, norm_ple = small
    m, d = x2.shape
    tm = TOKEN_TILE
    n_m = m // tm
    ple = p.shape[-1]
    assert m % tm == 0 and w_up.shape[-1] % FF_COLS == 0
    L = layer
    x_spec = pl.BlockSpec((tm, d), lambda i: (i, 0))
    slabs = [_slab_specs(w, L + 1, n_m, lambda i: i) for w in cast_next]
    outs = pl.pallas_call(
        functools.partial(_ffn_kernel, final_norm=final_norm, n_cast=len(cast_next)),
        grid=(n_m,),
        in_specs=[x_spec, pl.BlockSpec((None, tm, ple), lambda i: (L, i, 0)),
                  _resident(norm_mlp, L), _resident(w_up), _resident(w_down),
                  _resident(norm_ple, L), _resident(w_ple, L), _resident(w_ple_gate),
                  _resident(norm_final)] + [sl[0] for sl in slabs],
        out_specs=[x_spec] + [sl[1] for sl in slabs],
        out_shape=[jax.ShapeDtypeStruct(x2.shape, x2.dtype)] + [sl[2] for sl in slabs],
        scratch_shapes=[
            pltpu.VMEM((tm, d), BF16),
            pltpu.VMEM((tm, w_up.shape[-1]), BF16),
        ],
        name=f"ffn_l{layer}",
        compiler_params=pltpu.CompilerParams(
            dimension_semantics=("arbitrary",),
            vmem_limit_bytes=VMEM_LIMIT_BYTES),
    )(x2, p, norm_mlp, w_up, w_down, norm_ple, w_ple, w_ple_gate, norm_final, *cast_next)
    return outs[0], tuple(outs[1:])


def kernel(x, p, norm_mix, w_in, conf_dw, conf_dw_b, conf_ln_g, conf_ln_b, pool_w, pool_scale, sc_conv, gmlp_ln_g, gmlp_ln_b, gmlp_ws, gmlp_bs, w_branch, w_out, norm_mlp, w_up, w_down, norm_ple, w_ple, w_ple_gate, norm_final):
    depth = w_in.shape[0]
    b, s, d = x.shape
    row = lambda a: a[:, None, :]
    vectors = jnp.concatenate([norm_mix, conf_dw_b, conf_ln_g, conf_ln_b, pool_scale, gmlp_ln_g,
                               gmlp_ln_b], axis=1)
    mixer_small = (row(vectors), conf_dw, sc_conv, gmlp_ws, jnp.swapaxes(gmlp_bs, 1, 2))
    ffn_small = (row(norm_mlp), row(norm_ple))
    pool_w16 = pool_w.astype(BF16)
    w_ple16 = w_ple.astype(BF16)
    p2 = p.reshape(depth, b * s, p.shape[-1])
    nf = norm_final[None, :]
    mixer_w = (w_in[0].astype(BF16), w_branch[0].astype(BF16), w_out[0].astype(BF16))
    for i in range(depth):
        x, ffn_w = _mixer_call(i, x, mixer_small, *mixer_w, pool_w16, (w_up, w_down, w_ple_gate))
        last = i == depth - 1
        x2, mixer_w = _ffn_call(i, last, x.reshape(b * s, d), p2, ffn_small, *ffn_w, w_ple16, nf,
                                () if last else (w_in, w_branch, w_out))
        x = x2.reshape(b, s, d)
    return x
```

```python
import functools

import jax
import jax.numpy as jnp
from jax import lax
from jax.experimental import pallas as pl
from jax.experimental.pallas import tpu as pltpu

EPS = 1e-6
POOL_WINDOWS = (2, 4, 8, 16)
GMLP_CHUNK = 128
LANES = 128
BF16_SUBLANES = 16

VMEM_LIMIT_BYTES = 60 * 1024 * 1024

TIME_TILE = 512
TOKEN_TILE = 1024
CONV_ROWS = 32
MERGE_COLS = 256
FF_COLS = 512
OUT_COLS = 256
HALO_A = 32
HALO_P = 16
HALO_C = 8

BF16 = jnp.bfloat16
F32 = jnp.float32


def _dot(a, b):
    return jnp.dot(a, b, preferred_element_type=F32)


def _sigmoid(x):
    return 0.5 * jnp.tanh(0.5 * x) + 0.5


def _rms(x, g):
    ms = jnp.mean(x * x, axis=-1, keepdims=True)
    return x * lax.rsqrt(ms + EPS) * g


def _layer_norm(x, g, b):
    mu = jnp.mean(x, axis=-1, keepdims=True)
    xc = x - mu
    var = jnp.mean(xc * xc, axis=-1, keepdims=True)
    return xc * lax.rsqrt(var + EPS) * g + b


def _cast_slabs(src_refs, dst_refs):
    for src, dst in zip(src_refs, dst_refs, strict=True):
        dst[...] = src[...].astype(BF16)


def _carry_halo(t, buf, halo, rows):
    @pl.when(t == 0)
    def _():
        buf[:, 0:halo, :] = jnp.zeros((buf.shape[0], halo, buf.shape[2]), buf.dtype)

    @pl.when(t > 0)
    def _():
        buf[:, 0:halo, :] = buf[:, rows:rows + halo, :]


def _store_lane_groups(buf, halo, val):
    rows = val.shape[0]
    for j in range(buf.shape[0]):
        buf[j, halo:halo + rows, :] = val[:, j * LANES:(j + 1) * LANES]


def _mixer_kernel(x_ref, vec_ref, win_ref, dw_ref, pw_ref, sc_ref, ws_ref, bst_ref,
                  wbr_ref, wout_ref, c0_ref, c1_ref, c2_ref, o_ref, d0_ref, d1_ref, d2_ref,
                  h_scr, abuf, pbuf, cbuf, y_scr, g0_scr, part_scr, m_scr):
    t = pl.program_id(1)
    ts = x_ref.shape[1]
    d = x_ref.shape[2]
    n_lane_groups = abuf.shape[0]
    mix = n_lane_groups * LANES
    nrm_ref = vec_ref.at[:, 0:d]
    dwb_ref, lag_ref, lab_ref, ps_ref, ldg_ref, ldb_ref = (
        vec_ref.at[:, d + i * mix:d + (i + 1) * mix] for i in range(6))
    conv_k = dw_ref.shape[0]
    sc_k = sc_ref.shape[0]
    n_branch = wbr_ref.shape[0]
    col_a, col_b, col_c, col_d = 0, 2 * mix, 3 * mix, 6 * mix
    col_g = 8 * mix

    _carry_halo(t, abuf, HALO_A, ts)
    _carry_halo(t, pbuf, HALO_P, ts)
    _carry_halo(t, cbuf, HALO_C, ts)

    h_scr[...] = _rms(x_ref[0], nrm_ref[...]).astype(BF16)
    _cast_slabs((c0_ref, c1_ref, c2_ref), (d0_ref, d1_ref, d2_ref))

    pa = _dot(h_scr[...], win_ref[:, col_a:col_a + 2 * mix])
    _store_lane_groups(abuf, HALO_A, pa[:, :mix] * _sigmoid(pa[:, mix:]))

    for c in range(ts // CONV_ROWS):
        base = c * CONV_ROWS
        accs = []
        for j in range(n_lane_groups):
            ls = slice(j * LANES, (j + 1) * LANES)
            acc = jnp.broadcast_to(dwb_ref[:, ls], (CONV_ROWS, LANES))
            for k in range(conv_k):
                off = HALO_A - (conv_k - 1) + k
                acc = acc + dw_ref[k:k + 1, ls] * abuf[j, pl.ds(base + off, CONV_ROWS), :]
            accs.append(acc)
        y = _layer_norm(jnp.concatenate(accs, axis=-1), lag_ref[...], lab_ref[...])
        y_scr[0, pl.ds(base, CONV_ROWS), :] = (y * _sigmoid(y)).astype(BF16)

    _store_lane_groups(pbuf, HALO_P, _dot(h_scr[...], win_ref[:, col_b:col_b + mix]))
    pos = (t * ts + 1 + lax.broadcasted_iota(jnp.int32, (ts, 1), 0)).astype(F32)
    for gi, w in enumerate(POOL_WINDOWS):
        sl = slice(gi * LANES, (gi + 1) * LANES)
        u = pbuf[gi, HALO_P:HALO_P + ts, :]
        s = u
        for j in range(1, w):
            s = s + pbuf[gi, HALO_P - j:HALO_P - j + ts, :]
        pooled = s / jnp.minimum(pos, float(w)) - u
        mixed = _dot(pooled.astype(BF16), pw_ref[gi]) * ps_ref[:, sl]
        y_scr[1, :, sl] = mixed.astype(BF16)

    pc = _dot(h_scr[...], win_ref[:, col_c:col_c + 3 * mix])
    _store_lane_groups(cbuf, HALO_C, pc[:, mix:2 * mix] * pc[:, 2 * mix:])
    for j in range(n_lane_groups):
        ls = slice(j * LANES, (j + 1) * LANES)
        conv = None
        for k in range(sc_k):
            off = HALO_C - (sc_k - 1) + k
            term = sc_ref[k:k + 1, ls] * cbuf[j, off:off + ts, :]
            conv = term if conv is None else conv + term
        y_scr[2, :, ls] = (pc[:, ls] * conv).astype(BF16)

    pd = _dot(h_scr[...], win_ref[:, col_d:col_d + 2 * mix])
    vn = _layer_norm(pd[:, mix:], ldg_ref[...], ldb_ref[...]).astype(BF16)
    u = pd[:, :mix]
    tri = (lax.broadcasted_iota(jnp.int32, (GMLP_CHUNK, GMLP_CHUNK), 0)
           >= lax.broadcasted_iota(jnp.int32, (GMLP_CHUNK, GMLP_CHUNK), 1))
    n_groups = ws_ref.shape[0]
    gg = mix // n_groups
    for g in range(n_groups):
        wsm = jnp.where(tri, ws_ref[g], 0.0).astype(BF16)
        bcol = bst_ref[:, g:g + 1]
        cs = slice(g * gg, (g + 1) * gg)
        for c in range(ts // GMLP_CHUNK):
            rs = slice(c * GMLP_CHUNK, (c + 1) * GMLP_CHUNK)
            sg = _dot(wsm, vn[rs, cs]) + bcol
            y_scr[3, rs, cs] = (u[rs, cs] * sg).astype(BF16)

    def gate(k, n):
        gc = col_g + k * d + n * MERGE_COLS
        return _sigmoid(_dot(h_scr[...], win_ref[:, gc:gc + MERGE_COLS]))

    for n in range(d // MERGE_COLS):
        ns = slice(n * MERGE_COLS, (n + 1) * MERGE_COLS)
        g0_scr[:, ns] = gate(0, n)
        part = None
        for k in range(1, n_branch):
            term = gate(k, n) * _dot(y_scr[k], wbr_ref[k, :, ns])
            part = term if part is None else part + term
        part_scr[:, ns] = part

    for n in range(d // MERGE_COLS):
        ns = slice(n * MERGE_COLS, (n + 1) * MERGE_COLS)
        merged = part_scr[:, ns] + g0_scr[:, ns] * _dot(y_scr[0], wbr_ref[0, :, ns])
        m_scr[:, ns] = merged.astype(BF16)
    o_ref[0] = x_ref[0] + _dot(m_scr[...], wout_ref[...])


def _ffn_kernel(*refs, final_norm, n_cast):
    (x_ref, p_ref, nm_ref, wup_ref, wdn_ref, npl_ref, wple_ref, wpg_ref, nf_ref) = refs[:9]
    cast_src = refs[9:9 + n_cast]
    o_ref = refs[9 + n_cast]
    cast_dst = refs[10 + n_cast:10 + 2 * n_cast]
    h_scr, u_scr = refs[10 + 2 * n_cast:]
    d_ff = wup_ref.shape[1]
    _cast_slabs(cast_src, cast_dst)
    emb = _dot(p_ref[...].astype(BF16), wple_ref[...])
    x = x_ref[...]
    h_scr[...] = _rms(x, nm_ref[...]).astype(BF16)
    for f in range(d_ff // FF_COLS):
        fs = slice(f * FF_COLS, (f + 1) * FF_COLS)
        up = jnp.maximum(_dot(h_scr[...], wup_ref[:, fs]), 0.0)
        u_scr[:, fs] = (up * up).astype(BF16)
    x = x + _dot(u_scr[...], wdn_ref[...])
    h_scr[...] = _rms(x, npl_ref[...]).astype(BF16)
    if final_norm:
        gate = _sigmoid(_dot(h_scr[...], wpg_ref[...]))
        o_ref[...] = _rms(x + emb * gate, nf_ref[...])
    else:
        for n in range(x.shape[1] // OUT_COLS):
            ns = slice(n * OUT_COLS, (n + 1) * OUT_COLS)
            gate = _sigmoid(_dot(h_scr[...], wpg_ref[:, ns]))
            o_ref[:, ns] = x[:, ns] + emb[:, ns] * gate


def _resident(a, layer=None):
    if layer is None:
        return pl.BlockSpec(a.shape, lambda *_: (0,) * a.ndim, pipeline_mode=pl.Buffered(1))
    return pl.BlockSpec((None,) + tuple(a.shape[1:]), lambda *_: (layer,) + (0,) * (a.ndim - 1),
                        pipeline_mode=pl.Buffered(1))


def _slab_specs(stacked, layer, n_steps, step_of):
    shape = stacked.shape[1:]
    rows = shape[-2] // n_steps
    assert rows * n_steps == shape[-2] and rows % BF16_SUBLANES == 0, (shape, n_steps)
    block = shape[:-2] + (rows, shape[-1])
    lead = (0,) * (len(shape) - 2)
    in_spec = pl.BlockSpec((None,) + block, lambda *g: (layer,) + lead + (step_of(*g), 0))
    out_spec = pl.BlockSpec(block, lambda *g: lead + (step_of(*g), 0))
    return in_spec, out_spec, jax.ShapeDtypeStruct(shape, BF16)


def _mixer_call(layer, x, small, w_in, w_branch, w_out, pool_w, cast_next):
    vectors, conf_dw, sc_conv, gmlp_ws, gmlp_bs_t = small
    b, s, d = x.shape
    mix = conf_dw.shape[-1]
    assert vectors.shape[-1] == d + 6 * mix
    ts = TIME_TILE
    n_t = s // ts
    assert s % ts == 0 and ts % GMLP_CHUNK == 0 and ts % CONV_ROWS == 0
    assert conf_dw.shape[1] - 1 <= HALO_A and max(POOL_WINDOWS) - 1 <= HALO_P
    assert sc_conv.shape[1] - 1 <= HALO_C and d % MERGE_COLS == 0
    assert mix == len(POOL_WINDOWS) * LANES and pool_w.shape[-1] == LANES
    x_spec = pl.BlockSpec((1, ts, d), lambda bi, ti: (bi, ti, 0))
    L = layer
    in_specs = [
        x_spec, _resident(vectors, L), _resident(w_in),
        _resident(conf_dw, L), _resident(pool_w, L), _resident(sc_conv, L),
        _resident(gmlp_ws, L), _resident(gmlp_bs_t, L), _resident(w_branch), _resident(w_out)]
    slabs = [_slab_specs(w, L, b * n_t, lambda bi, ti: bi * n_t + ti) for w in cast_next]
    outs = pl.pallas_call(
        _mixer_kernel,
        grid=(b, n_t),
        in_specs=in_specs + [sl[0] for sl in slabs],
        out_specs=[x_spec] + [sl[1] for sl in slabs],
        out_shape=[jax.ShapeDtypeStruct(x.shape, x.dtype)] + [sl[2] for sl in slabs],
        scratch_shapes=[
            pltpu.VMEM((ts, d), BF16),
            pltpu.VMEM((mix // LANES, HALO_A + ts, LANES), F32),
            pltpu.VMEM((mix // LANES, HALO_P + ts, LANES), F32),
            pltpu.VMEM((mix // LANES, HALO_C + ts, LANES), F32),
            pltpu.VMEM((w_branch.shape[0], ts, mix), BF16),
            pltpu.VMEM((ts, d), F32),
            pltpu.VMEM((ts, d), F32),
            pltpu.VMEM((ts, d), BF16),
        ],
        name=f"mixer_l{layer}",
        compiler_params=pltpu.CompilerParams(
            dimension_semantics=("arbitrary", "arbitrary"),
            vmem_limit_bytes=VMEM_LIMIT_BYTES),
    )(x, vectors, w_in, conf_dw, pool_w, sc_conv, gmlp_ws, gmlp_bs_t, w_branch, w_out, *cast_next)
    return outs[0], tuple(outs[1:])


def _ffn_call(layer, final_norm, x2, p, small, w_up, w_down, w_ple_gate, w_ple, norm_final,
              cast_next):
    norm_mlp, norm_ple = small
    m, d = x2.shape
    tm = TOKEN_TILE
    n_m = m // tm
    ple = p.shape[-1]
    assert m % tm == 0 and w_up.shape[-1] % FF_COLS == 0
    L = layer
    x_spec = pl.BlockSpec((tm, d), lambda i: (i, 0))
    slabs = [_slab_specs(w, L + 1, n_m, lambda i: i) for w in cast_next]
    outs = pl.pallas_call(
        functools.partial(_ffn_kernel, final_norm=final_norm, n_cast=len(cast_next)),
        grid=(n_m,),
        in_specs=[x_spec, pl.BlockSpec((None, tm, ple), lambda i: (L, i, 0)),
                  _resident(norm_mlp, L), _resident(w_up), _resident(w_down),
                  _resident(norm_ple, L), _resident(w_ple, L), _resident(w_ple_gate),
                  _resident(norm_final)] + [sl[0] for sl in slabs],
        out_specs=[x_spec] + [sl[1] for sl in slabs],
        out_shape=[jax.ShapeDtypeStruct(x2.shape, x2.dtype)] + [sl[2] for sl in slabs],
        scratch_shapes=[
            pltpu.VMEM((tm, d), BF16),
            pltpu.VMEM((tm, w_up.shape[-1]), BF16),
        ],
        name=f"ffn_l{layer}",
        compiler_params=pltpu.CompilerParams(
            dimension_semantics=("arbitrary",),
            vmem_limit_bytes=VMEM_LIMIT_BYTES),
    )(x2, p, norm_mlp, w_up, w_down, norm_ple, w_ple, w_ple_gate, norm_final, *cast_next)
    return outs[0], tuple(outs[1:])


def kernel(x, p, norm_mix, w_in, conf_dw, conf_dw_b, conf_ln_g, conf_ln_b, pool_w, pool_scale, sc_conv, gmlp_ln_g, gmlp_ln_b, gmlp_ws, gmlp_bs, w_branch, w_out, norm_mlp, w_up, w_down, norm_ple, w_ple, w_ple_gate, norm_final):
    depth = w_in.shape[0]
    b, s, d = x.shape
    row = lambda a: a[:, None, :]
    vectors = jnp.concatenate([norm_mix, conf_dw_b, conf_ln_g, conf_ln_b, pool_scale, gmlp_ln_g,
                               gmlp_ln_b], axis=1)
    mixer_small = (row(vectors), conf_dw, sc_conv, gmlp_ws, jnp.swapaxes(gmlp_bs, 1, 2))
    ffn_small = (row(norm_mlp), row(norm_ple))
    pool_w16 = pool_w.astype(BF16)
    w_ple16 = w_ple.astype(BF16)
    p2 = p.reshape(depth, b * s, p.shape[-1])
    nf = norm_final[None, :]
    mixer_w = (w_in[0].astype(BF16), w_branch[0].astype(BF16), w_out[0].astype(BF16))
    for i in range(depth):
        x, ffn_w = _mixer_call(i, x, mixer_small, *mixer_w, pool_w16, (w_up, w_down, w_ple_gate))
        last = i == depth - 1
        x2, mixer_w = _ffn_call(i, last, x.reshape(b * s, d), p2, ffn_small, *ffn_w, w_ple16, nf,
                                () if last else (w_in, w_branch, w_out))
        x = x2.reshape(b, s, d)
    return x
```

```python
import functools

import jax
import jax.numpy as jnp
from jax import lax
from jax.experimental import pallas as pl
from jax.experimental.pallas import tpu as pltpu

EPS = 1e-6
POOL_WINDOWS = (2, 4, 8, 16)
GMLP_CHUNK = 128
LANES = 128
BF16_SUBLANES = 16

VMEM_LIMIT_BYTES = 60 * 1024 * 1024

TIME_TILE = 512
TOKEN_TILE = 1024
CONV_ROWS = 32
MERGE_COLS = 256
FF_COLS = 512
OUT_COLS = 256
STORED_GATES = 3
HALO_A = 32
HALO_P = 16
HALO_C = 8

BF16 = jnp.bfloat16
F32 = jnp.float32


def _dot(a, b):
    return jnp.dot(a, b, preferred_element_type=F32)


def _sigmoid(x):
    return 0.5 * jnp.tanh(0.5 * x) + 0.5


def _rms(x, g):
    ms = jnp.mean(x * x, axis=-1, keepdims=True)
    return x * lax.rsqrt(ms + EPS) * g


def _layer_norm(x, g, b):
    mu = jnp.mean(x, axis=-1, keepdims=True)
    xc = x - mu
    var = jnp.mean(xc * xc, axis=-1, keepdims=True)
    return xc * lax.rsqrt(var + EPS) * g + b


def _cast_slabs(src_refs, dst_refs):
    for src, dst in zip(src_refs, dst_refs, strict=True):
        dst[...] = src[...].astype(BF16)


def _carry_halo(t, buf, halo, rows):
    @pl.when(t == 0)
    def _():
        buf[:, 0:halo, :] = jnp.zeros((buf.shape[0], halo, buf.shape[2]), buf.dtype)

    @pl.when(t > 0)
    def _():
        buf[:, 0:halo, :] = buf[:, rows:rows + halo, :]


def _store_lane_groups(buf, halo, val):
    rows = val.shape[0]
    for j in range(buf.shape[0]):
        buf[j, halo:halo + rows, :] = val[:, j * LANES:(j + 1) * LANES]


def _mixer_kernel(x_ref, vec_ref, win_ref, dw_ref, pw_ref, sc_ref, ws_ref, bst_ref,
                  wbr_ref, wout_ref, c0_ref, c1_ref, c2_ref, o_ref, d0_ref, d1_ref, d2_ref,
                  h_scr, abuf, pbuf, cbuf, y_scr, g_scr, part_scr, m_scr):
    t = pl.program_id(1)
    ts = x_ref.shape[1]
    d = x_ref.shape[2]
    n_lane_groups = abuf.shape[0]
    mix = n_lane_groups * LANES
    nrm_ref = vec_ref.at[:, 0:d]
    dwb_ref, lag_ref, lab_ref, ps_ref, ldg_ref, ldb_ref = (
        vec_ref.at[:, d + i * mix:d + (i + 1) * mix] for i in range(6))
    conv_k = dw_ref.shape[0]
    sc_k = sc_ref.shape[0]
    n_branch = wbr_ref.shape[0]
    col_a, col_b, col_c, col_d = 0, 2 * mix, 3 * mix, 6 * mix
    col_g = 8 * mix

    _carry_halo(t, abuf, HALO_A, ts)
    _carry_halo(t, pbuf, HALO_P, ts)
    _carry_halo(t, cbuf, HALO_C, ts)

    h_scr[...] = _rms(x_ref[0], nrm_ref[...]).astype(BF16)
    _cast_slabs((c0_ref, c1_ref, c2_ref), (d0_ref, d1_ref, d2_ref))

    def gate(k, n):
        gc = col_g + k * d + n * MERGE_COLS
        return _sigmoid(_dot(h_scr[...], win_ref[:, gc:gc + MERGE_COLS]))

    def gates_to_scratch(k):
        for n in range(d // MERGE_COLS):
            g_scr[k, :, n * MERGE_COLS:(n + 1) * MERGE_COLS] = gate(k, n)

    pa = _dot(h_scr[...], win_ref[:, col_a:col_a + 2 * mix])
    _store_lane_groups(abuf, HALO_A, pa[:, :mix] * _sigmoid(pa[:, mix:]))

    for c in range(ts // CONV_ROWS):
        base = c * CONV_ROWS
        accs = []
        for j in range(n_lane_groups):
            ls = slice(j * LANES, (j + 1) * LANES)
            acc = jnp.broadcast_to(dwb_ref[:, ls], (CONV_ROWS, LANES))
            for k in range(conv_k):
                off = HALO_A - (conv_k - 1) + k
                acc = acc + dw_ref[k:k + 1, ls] * abuf[j, pl.ds(base + off, CONV_ROWS), :]
            accs.append(acc)
        y = _layer_norm(jnp.concatenate(accs, axis=-1), lag_ref[...], lab_ref[...])
        y_scr[0, pl.ds(base, CONV_ROWS), :] = (y * _sigmoid(y)).astype(BF16)

    _store_lane_groups(pbuf, HALO_P, _dot(h_scr[...], win_ref[:, col_b:col_b + mix]))
    gates_to_scratch(1)
    pos = (t * ts + 1 + lax.broadcasted_iota(jnp.int32, (ts, 1), 0)).astype(F32)
    for gi, w in enumerate(POOL_WINDOWS):
        sl = slice(gi * LANES, (gi + 1) * LANES)
        u = pbuf[gi, HALO_P:HALO_P + ts, :]
        s = u
        for j in range(1, w):
            s = s + pbuf[gi, HALO_P - j:HALO_P - j + ts, :]
        pooled = s / jnp.minimum(pos, float(w)) - u
        mixed = _dot(pooled.astype(BF16), pw_ref[gi]) * ps_ref[:, sl]
        y_scr[1, :, sl] = mixed.astype(BF16)

    pc = _dot(h_scr[...], win_ref[:, col_c:col_c + 3 * mix])
    gates_to_scratch(2)
    _store_lane_groups(cbuf, HALO_C, pc[:, mix:2 * mix] * pc[:, 2 * mix:])
    for j in range(n_lane_groups):
        ls = slice(j * LANES, (j + 1) * LANES)
        conv = None
        for k in range(sc_k):
            off = HALO_C - (sc_k - 1) + k
            term = sc_ref[k:k + 1, ls] * cbuf[j, off:off + ts, :]
            conv = term if conv is None else conv + term
        y_scr[2, :, ls] = (pc[:, ls] * conv).astype(BF16)

    pd = _dot(h_scr[...], win_ref[:, col_d:col_d + 2 * mix])
    gates_to_scratch(0)
    vn = _layer_norm(pd[:, mix:], ldg_ref[...], ldb_ref[...]).astype(BF16)
    u = pd[:, :mix]
    tri = (lax.broadcasted_iota(jnp.int32, (GMLP_CHUNK, GMLP_CHUNK), 0)
           >= lax.broadcasted_iota(jnp.int32, (GMLP_CHUNK, GMLP_CHUNK), 1))
    n_groups = ws_ref.shape[0]
    gg = mix // n_groups
    for g in range(n_groups):
        wsm = jnp.where(tri, ws_ref[g], 0.0).astype(BF16)
        bcol = bst_ref[:, g:g + 1]
        cs = slice(g * gg, (g + 1) * gg)
        for c in range(ts // GMLP_CHUNK):
            rs = slice(c * GMLP_CHUNK, (c + 1) * GMLP_CHUNK)
            sg = _dot(wsm, vn[rs, cs]) + bcol
            y_scr[3, rs, cs] = (u[rs, cs] * sg).astype(BF16)

    n_stored = g_scr.shape[0]
    assert n_stored == 3
    for n in range(d // MERGE_COLS):
        ns = slice(n * MERGE_COLS, (n + 1) * MERGE_COLS)
        part = None
        for k in range(1, n_branch):
            gk = g_scr[k, :, ns] if k < n_stored else gate(k, n)
            term = gk * _dot(y_scr[k], wbr_ref[k, :, ns])
            part = term if part is None else part + term
        part_scr[:, ns] = part

    for n in range(d // MERGE_COLS):
        ns = slice(n * MERGE_COLS, (n + 1) * MERGE_COLS)
        merged = part_scr[:, ns] + g_scr[0, :, ns] * _dot(y_scr[0], wbr_ref[0, :, ns])
        m_scr[:, ns] = merged.astype(BF16)
    o_ref[0] = x_ref[0] + _dot(m_scr[...], wout_ref[...])


def _ffn_kernel(*refs, final_norm, n_cast):
    (x_ref, p_ref, nm_ref, wup_ref, wdn_ref, npl_ref, wple_ref, wpg_ref, nf_ref) = refs[:9]
    cast_src = refs[9:9 + n_cast]
    o_ref = refs[9 + n_cast]
    cast_dst = refs[10 + n_cast:10 + 2 * n_cast]
    h_scr, u_scr = refs[10 + 2 * n_cast:]
    d_ff = wup_ref.shape[1]
    _cast_slabs(cast_src, cast_dst)
    emb = _dot(p_ref[...].astype(BF16), wple_ref[...])
    x = x_ref[...]
    h_scr[...] = _rms(x, nm_ref[...]).astype(BF16)
    for f in range(d_ff // FF_COLS):
        fs = slice(f * FF_COLS, (f + 1) * FF_COLS)
        up = jnp.maximum(_dot(h_scr[...], wup_ref[:, fs]), 0.0)
        u_scr[:, fs] = (up * up).astype(BF16)
    x = x + _dot(u_scr[...], wdn_ref[...])
    h_scr[...] = _rms(x, npl_ref[...]).astype(BF16)
    if final_norm:
        gate = _sigmoid(_dot(h_scr[...], wpg_ref[...]))
        o_ref[...] = _rms(x + emb * gate, nf_ref[...])
    else:
        for n in range(x.shape[1] // OUT_COLS):
            ns = slice(n * OUT_COLS, (n + 1) * OUT_COLS)
            gate = _sigmoid(_dot(h_scr[...], wpg_ref[:, ns]))
            o_ref[:, ns] = x[:, ns] + emb[:, ns] * gate


def _resident(a, layer=None):
    if layer is None:
        return pl.BlockSpec(a.shape, lambda *_: (0,) * a.ndim, pipeline_mode=pl.Buffered(1))
    return pl.BlockSpec((None,) + tuple(a.shape[1:]), lambda *_: (layer,) + (0,) * (a.ndim - 1),
                        pipeline_mode=pl.Buffered(1))


def _slab_specs(stacked, layer, n_steps, step_of):
    shape = stacked.shape[1:]
    rows = shape[-2] // n_steps
    assert rows * n_steps == shape[-2] and rows % BF16_SUBLANES == 0, (shape, n_steps)
    block = shape[:-2] + (rows, shape[-1])
    lead = (0,) * (len(shape) - 2)
    in_spec = pl.BlockSpec((None,) + block, lambda *g: (layer,) + lead + (step_of(*g), 0))
    out_spec = pl.BlockSpec(block, lambda *g: lead + (step_of(*g), 0))
    return in_spec, out_spec, jax.ShapeDtypeStruct(shape, BF16)


def _mixer_call(layer, x, small, w_in, w_branch, w_out, pool_w, cast_next):
    vectors, conf_dw, sc_conv, gmlp_ws, gmlp_bs_t = small
    b, s, d = x.shape
    mix = conf_dw.shape[-1]
    assert vectors.shape[-1] == d + 6 * mix
    ts = TIME_TILE
    n_t = s // ts
    assert s % ts == 0 and ts % GMLP_CHUNK == 0 and ts % CONV_ROWS == 0
    assert conf_dw.shape[1] - 1 <= HALO_A and max(POOL_WINDOWS) - 1 <= HALO_P
    assert sc_conv.shape[1] - 1 <= HALO_C and d % MERGE_COLS == 0
    assert mix == len(POOL_WINDOWS) * LANES and pool_w.shape[-1] == LANES
    x_spec = pl.BlockSpec((1, ts, d), lambda bi, ti: (bi, ti, 0))
    L = layer
    in_specs = [
        x_spec, _resident(vectors, L), _resident(w_in),
        _resident(conf_dw, L), _resident(pool_w, L), _resident(sc_conv, L),
        _resident(gmlp_ws, L), _resident(gmlp_bs_t, L), _resident(w_branch), _resident(w_out)]
    slabs = [_slab_specs(w, L, b * n_t, lambda bi, ti: bi * n_t + ti) for w in cast_next]
    outs = pl.pallas_call(
        _mixer_kernel,
        grid=(b, n_t),
        in_specs=in_specs + [sl[0] for sl in slabs],
        out_specs=[x_spec] + [sl[1] for sl in slabs],
        out_shape=[jax.ShapeDtypeStruct(x.shape, x.dtype)] + [sl[2] for sl in slabs],
        scratch_shapes=[
            pltpu.VMEM((ts, d), BF16),
            pltpu.VMEM((mix // LANES, HALO_A + ts, LANES), F32),
            pltpu.VMEM((mix // LANES, HALO_P + ts, LANES), F32),
            pltpu.VMEM((mix // LANES, HALO_C + ts, LANES), F32),
            pltpu.VMEM((w_branch.shape[0], ts, mix), BF16),
            pltpu.VMEM((STORED_GATES, ts, d), F32),
            pltpu.VMEM((ts, d), F32),
            pltpu.VMEM((ts, d), BF16),
        ],
        name=f"mixer_l{layer}",
        compiler_params=pltpu.CompilerParams(
            dimension_semantics=("arbitrary", "arbitrary"),
            vmem_limit_bytes=VMEM_LIMIT_BYTES),
    )(x, vectors, w_in, conf_dw, pool_w, sc_conv, gmlp_ws, gmlp_bs_t, w_branch, w_out, *cast_next)
    return outs[0], tuple(outs[1:])


def _ffn_call(layer, final_norm, x2, p, small, w_up, w_down, w_ple_gate, w_ple, norm_final,
              cast_next):
    norm_mlp, norm_ple = small
    m, d = x2.shape
    tm = TOKEN_TILE
    n_m = m // tm
    ple = p.shape[-1]
    assert m % tm == 0 and w_up.shape[-1] % FF_COLS == 0
    L = layer
    x_spec = pl.BlockSpec((tm, d), lambda i: (i, 0))
    slabs = [_slab_specs(w, L + 1, n_m, lambda i: i) for w in cast_next]
    outs = pl.pallas_call(
        functools.partial(_ffn_kernel, final_norm=final_norm, n_cast=len(cast_next)),
        grid=(n_m,),
        in_specs=[x_spec, pl.BlockSpec((None, tm, ple), lambda i: (L, i, 0)),
                  _resident(norm_mlp, L), _resident(w_up), _resident(w_down),
                  _resident(norm_ple, L), _resident(w_ple, L), _resident(w_ple_gate),
                  _resident(norm_final)] + [sl[0] for sl in slabs],
        out_specs=[x_spec] + [sl[1] for sl in slabs],
        out_shape=[jax.ShapeDtypeStruct(x2.shape, x2.dtype)] + [sl[2] for sl in slabs],
        scratch_shapes=[
            pltpu.VMEM((tm, d), BF16),
            pltpu.VMEM((tm, w_up.shape[-1]), BF16),
        ],
        name=f"ffn_l{layer}",
        compiler_params=pltpu.CompilerParams(
            dimension_semantics=("arbitrary",),
            vmem_limit_bytes=VMEM_LIMIT_BYTES),
    )(x2, p, norm_mlp, w_up, w_down, norm_ple, w_ple, w_ple_gate, norm_final, *cast_next)
    return outs[0], tuple(outs[1:])


def kernel(x, p, norm_mix, w_in, conf_dw, conf_dw_b, conf_ln_g, conf_ln_b, pool_w, pool_scale, sc_conv, gmlp_ln_g, gmlp_ln_b, gmlp_ws, gmlp_bs, w_branch, w_out, norm_mlp, w_up, w_down, norm_ple, w_ple, w_ple_gate, norm_final):
    depth = w_in.shape[0]
    b, s, d = x.shape
    row = lambda a: a[:, None, :]
    vectors = jnp.concatenate([norm_mix, conf_dw_b, conf_ln_g, conf_ln_b, pool_scale, gmlp_ln_g,
                               gmlp_ln_b], axis=1)
    mixer_small = (row(vectors), conf_dw, sc_conv, gmlp_ws, jnp.swapaxes(gmlp_bs, 1, 2))
    ffn_small = (row(norm_mlp), row(norm_ple))
    pool_w16 = pool_w.astype(BF16)
    w_ple16 = w_ple.astype(BF16)
    p2 = p.reshape(depth, b * s, p.shape[-1])
    nf = norm_final[None, :]
    mixer_w = (w_in[0].astype(BF16), w_branch[0].astype(BF16), w_out[0].astype(BF16))
    for i in range(depth):
        x, ffn_w = _mixer_call(i, x, mixer_small, *mixer_w, pool_w16, (w_up, w_down, w_ple_gate))
        last = i == depth - 1
        x2, mixer_w = _ffn_call(i, last, x.reshape(b * s, d), p2, ffn_small, *ffn_w, w_ple16, nf,
                                () if last else (w_in, w_branch, w_out))
        x = x2.reshape(b, s, d)
    return x
```

```python
import functools

import jax
import jax.numpy as jnp
from jax import lax
from jax.experimental import pallas as pl
from jax.experimental.pallas import tpu as pltpu

EPS = 1e-6
POOL_WINDOWS = (2, 4, 8, 16)
GMLP_CHUNK = 128
LANES = 128
BF16_SUBLANES = 16

VMEM_LIMIT_BYTES = 60 * 1024 * 1024

TIME_TILE = 512
TOKEN_TILE = 1024
CONV_ROWS = 32
MERGE_COLS = 256
FF_COLS = 512
OUT_COLS = 256
STORED_GATES = 3
HALO_A = 32
HALO_P = 16
HALO_C = 8

BF16 = jnp.bfloat16
F32 = jnp.float32


def _dot(a, b):
    return jnp.dot(a, b, preferred_element_type=F32)


def _sigmoid(x):
    return 0.5 * jnp.tanh(0.5 * x) + 0.5


def _rms(x, g):
    ms = jnp.mean(x * x, axis=-1, keepdims=True)
    return x * lax.rsqrt(ms + EPS) * g


def _layer_norm(x, g, b):
    mu = jnp.mean(x, axis=-1, keepdims=True)
    xc = x - mu
    var = jnp.mean(xc * xc, axis=-1, keepdims=True)
    return xc * lax.rsqrt(var + EPS) * g + b


def _cast_slabs(src_refs, dst_refs):
    for src, dst in zip(src_refs, dst_refs, strict=True):
        dst[...] = src[...].astype(BF16)


def _carry_halo(t, buf, halo, rows):
    @pl.when(t == 0)
    def _():
        buf[:, 0:halo, :] = jnp.zeros((buf.shape[0], halo, buf.shape[2]), buf.dtype)

    @pl.when(t > 0)
    def _():
        buf[:, 0:halo, :] = buf[:, rows:rows + halo, :]


def _store_lane_groups(buf, halo, val):
    rows = val.shape[0]
    for j in range(buf.shape[0]):
        buf[j, halo:halo + rows, :] = val[:, j * LANES:(j + 1) * LANES]


def _mixer_kernel(x_ref, vec_ref, win_ref, dw_ref, pw_ref, sc_ref, ws_ref, bst_ref,
                  wbr_ref, wout_ref, c0_ref, c1_ref, c2_ref, o_ref, d0_ref, d1_ref, d2_ref,
                  h_scr, abuf, pbuf, cbuf, y_scr, g_scr, part_scr, m_scr):
    t = pl.program_id(1)
    ts = x_ref.shape[1]
    d = x_ref.shape[2]
    n_lane_groups = abuf.shape[0]
    mix = n_lane_groups * LANES
    nrm_ref = vec_ref.at[:, 0:d]
    dwb_ref, lag_ref, lab_ref, ps_ref, ldg_ref, ldb_ref = (
        vec_ref.at[:, d + i * mix:d + (i + 1) * mix] for i in range(6))
    conv_k = dw_ref.shape[0]
    sc_k = sc_ref.shape[0]
    n_branch = wbr_ref.shape[0]
    col_a, col_b, col_c, col_d = 0, 2 * mix, 3 * mix, 6 * mix
    col_g = 8 * mix

    _carry_halo(t, abuf, HALO_A, ts)
    _carry_halo(t, pbuf, HALO_P, ts)
    _carry_halo(t, cbuf, HALO_C, ts)

    _cast_slabs((c0_ref, c1_ref, c2_ref), (d0_ref, d1_ref, d2_ref))

    def gate(k, n):
        gc = col_g + k * d + n * MERGE_COLS
        return _sigmoid(_dot(h_scr[...], win_ref[:, gc:gc + MERGE_COLS]))

    def gates_to_scratch(k):
        for n in range(d // MERGE_COLS):
            g_scr[k, :, n * MERGE_COLS:(n + 1) * MERGE_COLS] = gate(k, n)

    half = ts // 2
    for r in range(2):
        rs = slice(r * half, (r + 1) * half)
        h_scr[rs, :] = _rms(x_ref[0, rs, :], nrm_ref[...]).astype(BF16)
        pa = _dot(h_scr[rs, :], win_ref[:, col_a:col_a + 2 * mix])
        _store_lane_groups(abuf, HALO_A + r * half, pa[:, :mix] * _sigmoid(pa[:, mix:]))

    for c in range(ts // CONV_ROWS):
        base = c * CONV_ROWS
        accs = []
        for j in range(n_lane_groups):
            ls = slice(j * LANES, (j + 1) * LANES)
            acc = jnp.broadcast_to(dwb_ref[:, ls], (CONV_ROWS, LANES))
            for k in range(conv_k):
                off = HALO_A - (conv_k - 1) + k
                acc = acc + dw_ref[k:k + 1, ls] * abuf[j, pl.ds(base + off, CONV_ROWS), :]
            accs.append(acc)
        y = _layer_norm(jnp.concatenate(accs, axis=-1), lag_ref[...], lab_ref[...])
        y_scr[0, pl.ds(base, CONV_ROWS), :] = (y * _sigmoid(y)).astype(BF16)

    _store_lane_groups(pbuf, HALO_P, _dot(h_scr[...], win_ref[:, col_b:col_b + mix]))
    gates_to_scratch(1)
    pos = (t * ts + 1 + lax.broadcasted_iota(jnp.int32, (ts, 1), 0)).astype(F32)
    for gi, w in enumerate(POOL_WINDOWS):
        sl = slice(gi * LANES, (gi + 1) * LANES)
        u = pbuf[gi, HALO_P:HALO_P + ts, :]
        s = u
        for j in range(1, w):
            s = s + pbuf[gi, HALO_P - j:HALO_P - j + ts, :]
        pooled = s / jnp.minimum(pos, float(w)) - u
        mixed = _dot(pooled.astype(BF16), pw_ref[gi]) * ps_ref[:, sl]
        y_scr[1, :, sl] = mixed.astype(BF16)

    pc = _dot(h_scr[...], win_ref[:, col_c:col_c + 3 * mix])
    gates_to_scratch(2)
    _store_lane_groups(cbuf, HALO_C, pc[:, mix:2 * mix] * pc[:, 2 * mix:])
    for j in range(n_lane_groups):
        ls = slice(j * LANES, (j + 1) * LANES)
        conv = None
        for k in range(sc_k):
            off = HALO_C - (sc_k - 1) + k
            term = sc_ref[k:k + 1, ls] * cbuf[j, off:off + ts, :]
            conv = term if conv is None else conv + term
        y_scr[2, :, ls] = (pc[:, ls] * conv).astype(BF16)

    pd = _dot(h_scr[...], win_ref[:, col_d:col_d + 2 * mix])
    gates_to_scratch(0)
    vn = _layer_norm(pd[:, mix:], ldg_ref[...], ldb_ref[...]).astype(BF16)
    u = pd[:, :mix]
    tri = (lax.broadcasted_iota(jnp.int32, (GMLP_CHUNK, GMLP_CHUNK), 0)
           >= lax.broadcasted_iota(jnp.int32, (GMLP_CHUNK, GMLP_CHUNK), 1))
    n_groups = ws_ref.shape[0]
    gg = mix // n_groups
    for g in range(n_groups):
        wsm = jnp.where(tri, ws_ref[g], 0.0).astype(BF16)
        bcol = bst_ref[:, g:g + 1]
        cs = slice(g * gg, (g + 1) * gg)
        for c in range(ts // GMLP_CHUNK):
            rs = slice(c * GMLP_CHUNK, (c + 1) * GMLP_CHUNK)
            sg = _dot(wsm, vn[rs, cs]) + bcol
            y_scr[3, rs, cs] = (u[rs, cs] * sg).astype(BF16)

    n_stored = g_scr.shape[0]
    assert n_stored == 3
    for n in range(d // MERGE_COLS):
        ns = slice(n * MERGE_COLS, (n + 1) * MERGE_COLS)
        part = None
        for k in range(1, n_branch):
            gk = g_scr[k, :, ns] if k < n_stored else gate(k, n)
            term = gk * _dot(y_scr[k], wbr_ref[k, :, ns])
            part = term if part is None else part + term
        part_scr[:, ns] = part

    for n in range(d // MERGE_COLS):
        ns = slice(n * MERGE_COLS, (n + 1) * MERGE_COLS)
        merged = part_scr[:, ns] + g_scr[0, :, ns] * _dot(y_scr[0], wbr_ref[0, :, ns])
        m_scr[:, ns] = merged.astype(BF16)
    o_ref[0] = x_ref[0] + _dot(m_scr[...], wout_ref[...])


def _ffn_kernel(*refs, final_norm, n_cast):
    (x_ref, p_ref, nm_ref, wup_ref, wdn_ref, npl_ref, wple_ref, wpg_ref, nf_ref) = refs[:9]
    cast_src = refs[9:9 + n_cast]
    o_ref = refs[9 + n_cast]
    cast_dst = refs[10 + n_cast:10 + 2 * n_cast]
    h_scr, u_scr = refs[10 + 2 * n_cast:]
    d_ff = wup_ref.shape[1]
    _cast_slabs(cast_src, cast_dst)
    emb = _dot(p_ref[...].astype(BF16), wple_ref[...])
    x = x_ref[...]
    h_scr[...] = _rms(x, nm_ref[...]).astype(BF16)
    for f in range(d_ff // FF_COLS):
        fs = slice(f * FF_COLS, (f + 1) * FF_COLS)
        up = jnp.maximum(_dot(h_scr[...], wup_ref[:, fs]), 0.0)
        u_scr[:, fs] = (up * up).astype(BF16)
    x = x + _dot(u_scr[...], wdn_ref[...])
    h_scr[...] = _rms(x, npl_ref[...]).astype(BF16)
    if final_norm:
        gate = _sigmoid(_dot(h_scr[...], wpg_ref[...]))
        o_ref[...] = _rms(x + emb * gate, nf_ref[...])
    else:
        for n in range(x.shape[1] // OUT_COLS):
            ns = slice(n * OUT_COLS, (n + 1) * OUT_COLS)
            gate = _sigmoid(_dot(h_scr[...], wpg_ref[:, ns]))
            o_ref[:, ns] = x[:, ns] + emb[:, ns] * gate


def _resident(a, layer=None):
    if layer is None:
        return pl.BlockSpec(a.shape, lambda *_: (0,) * a.ndim, pipeline_mode=pl.Buffered(1))
    return pl.BlockSpec((None,) + tuple(a.shape[1:]), lambda *_: (layer,) + (0,) * (a.ndim - 1),
                        pipeline_mode=pl.Buffered(1))


def _slab_specs(stacked, layer, n_steps, step_of):
    shape = stacked.shape[1:]
    rows = shape[-2] // n_steps
    assert rows * n_steps == shape[-2] and rows % BF16_SUBLANES == 0, (shape, n_steps)
    block = shape[:-2] + (rows, shape[-1])
    lead = (0,) * (len(shape) - 2)
    in_spec = pl.BlockSpec((None,) + block, lambda *g: (layer,) + lead + (step_of(*g), 0))
    out_spec = pl.BlockSpec(block, lambda *g: lead + (step_of(*g), 0))
    return in_spec, out_spec, jax.ShapeDtypeStruct(shape, BF16)


def _mixer_call(layer, x, small, w_in, w_branch, w_out, pool_w, cast_next):
    vectors, conf_dw, sc_conv, gmlp_ws, gmlp_bs_t = small
    b, s, d = x.shape
    mix = conf_dw.shape[-1]
    assert vectors.shape[-1] == d + 6 * mix
    ts = TIME_TILE
    n_t = s // ts
    assert s % ts == 0 and ts % GMLP_CHUNK == 0 and ts % CONV_ROWS == 0
    assert conf_dw.shape[1] - 1 <= HALO_A and max(POOL_WINDOWS) - 1 <= HALO_P
    assert sc_conv.shape[1] - 1 <= HALO_C and d % MERGE_COLS == 0
    assert mix == len(POOL_WINDOWS) * LANES and pool_w.shape[-1] == LANES
    x_spec = pl.BlockSpec((1, ts, d), lambda bi, ti: (bi, ti, 0))
    L = layer
    in_specs = [
        x_spec, _resident(vectors, L), _resident(w_in),
        _resident(conf_dw, L), _resident(pool_w, L), _resident(sc_conv, L),
        _resident(gmlp_ws, L), _resident(gmlp_bs_t, L), _resident(w_branch), _resident(w_out)]
    slabs = [_slab_specs(w, L, b * n_t, lambda bi, ti: bi * n_t + ti) for w in cast_next]
    outs = pl.pallas_call(
        _mixer_kernel,
        grid=(b, n_t),
        in_specs=in_specs + [sl[0] for sl in slabs],
        out_specs=[x_spec] + [sl[1] for sl in slabs],
        out_shape=[jax.ShapeDtypeStruct(x.shape, x.dtype)] + [sl[2] for sl in slabs],
        scratch_shapes=[
            pltpu.VMEM((ts, d), BF16),
            pltpu.VMEM((mix // LANES, HALO_A + ts, LANES), F32),
            pltpu.VMEM((mix // LANES, HALO_P + ts, LANES), F32),
            pltpu.VMEM((mix // LANES, HALO_C + ts, LANES), F32),
            pltpu.VMEM((w_branch.shape[0], ts, mix), BF16),
            pltpu.VMEM((STORED_GATES, ts, d), F32),
            pltpu.VMEM((ts, d), F32),
            pltpu.VMEM((ts, d), BF16),
        ],
        name=f"mixer_l{layer}",
        compiler_params=pltpu.CompilerParams(
            dimension_semantics=("arbitrary", "arbitrary"),
            vmem_limit_bytes=VMEM_LIMIT_BYTES),
    )(x, vectors, w_in, conf_dw, pool_w, sc_conv, gmlp_ws, gmlp_bs_t, w_branch, w_out, *cast_next)
    return outs[0], tuple(outs[1:])


def _ffn_call(layer, final_norm, x2, p, small, w_up, w_down, w_ple_gate, w_ple, norm_final,
              cast_next):
    norm_mlp, norm_ple = small
    m, d = x2.shape
    tm = TOKEN_TILE
    n_m = m // tm
    ple = p.shape[-1]
    assert m % tm == 0 and w_up.shape[-1] % FF_COLS == 0
    L = layer
    x_spec = pl.BlockSpec((tm, d), lambda i: (i, 0))
    slabs = [_slab_specs(w, L + 1, n_m, lambda i: i) for w in cast_next]
    outs = pl.pallas_call(
        functools.partial(_ffn_kernel, final_norm=final_norm, n_cast=len(cast_next)),
        grid=(n_m,),
        in_specs=[x_spec, pl.BlockSpec((None, tm, ple), lambda i: (L, i, 0)),
                  _resident(norm_mlp, L), _resident(w_up), _resident(w_down),
                  _resident(norm_ple, L), _resident(w_ple, L), _resident(w_ple_gate),
                  _resident(norm_final)] + [sl[0] for sl in slabs],
        out_specs=[x_spec] + [sl[1] for sl in slabs],
        out_shape=[jax.ShapeDtypeStruct(x2.shape, x2.dtype)] + [sl[2] for sl in slabs],
        scratch_shapes=[
            pltpu.VMEM((tm, d), BF16),
            pltpu.VMEM((tm, w_up.shape[-1]), BF16),
        ],
        name=f"ffn_l{layer}",
        compiler_params=pltpu.CompilerParams(
            dimension_semantics=("arbitrary",),
            vmem_limit_bytes=VMEM_LIMIT_BYTES),
    )(x2, p, norm_mlp, w_up, w_down, norm_ple, w_ple, w_ple_gate, norm_final, *cast_next)
    return outs[0], tuple(outs[1:])


def kernel(x, p, norm_mix, w_in, conf_dw, conf_dw_b, conf_ln_g, conf_ln_b, pool_w, pool_scale, sc_conv, gmlp_ln_g, gmlp_ln_b, gmlp_ws, gmlp_bs, w_branch, w_out, norm_mlp, w_up, w_down, norm_ple, w_ple, w_ple_gate, norm_final):
    depth = w_in.shape[0]
    b, s, d = x.shape
    row = lambda a: a[:, None, :]
    vectors = jnp.concatenate([norm_mix, conf_dw_b, conf_ln_g, conf_ln_b, pool_scale, gmlp_ln_g,
                               gmlp_ln_b], axis=1)
    mixer_small = (row(vectors), conf_dw, sc_conv, gmlp_ws, jnp.swapaxes(gmlp_bs, 1, 2))
    ffn_small = (row(norm_mlp), row(norm_ple))
    pool_w16 = pool_w.astype(BF16)
    w_ple16 = w_ple.astype(BF16)
    p2 = p.reshape(depth, b * s, p.shape[-1])
    nf = norm_final[None, :]
    mixer_w = (w_in[0].astype(BF16), w_branch[0].astype(BF16), w_out[0].astype(BF16))
    for i in range(depth):
        x, ffn_w = _mixer_call(i, x, mixer_small, *mixer_w, pool_w16, (w_up, w_down, w_ple_gate))
        last = i == depth - 1
        x2, mixer_w = _ffn_call(i, last, x.reshape(b * s, d), p2, ffn_small, *ffn_w, w_ple16, nf,
                                () if last else (w_in, w_branch, w_out))
        x = x2.reshape(b, s, d)
    return x
```
